```python
import jax
import jax.numpy as jnp
from jax import lax
import numpy as np

D_MODEL = 1024
BATCH = 16
SEQ = 2048
DEPTH = 2
DEC_BATCH = 8
DEC_SEQ = 64
PAST_LEN = 2048

CHUNK = 64
N_META = 16
EPS = 1e-6

R_HEADS = 4
R_DK = 64
R_DV = 128
ROPE_BASE = 10000.0
M_HEADS = 4
M_DH = 128
CONV_W = 4
G_HEADS = 4
G_DK = 64
G_DV = 128
G_RANK = 16
G_NORMALIZER = 16.0

R_QK = R_HEADS * R_DK
R_V = R_HEADS * R_DV
M_W = M_HEADS * M_DH
G_QK = G_HEADS * G_DK
G_V = G_HEADS * G_DV
D_FF = ((-(-8 * D_MODEL // 3)) + 255) // 256 * 256
IN_WIDTHS = (R_QK, R_QK, R_V, R_V, M_W, M_W, M_HEADS, M_HEADS,
             G_QK, G_QK, G_V, G_V, G_RANK, D_MODEL, D_MODEL, D_MODEL)
D_IN = sum(IN_WIDTHS)

kernel_name = 'hybrid_streaming_retention_mlstm_gla_step'


def _rmsnorm(x, g):
    xf = x.astype(jnp.float32)
    y = xf * lax.rsqrt(jnp.mean(xf * xf, axis=-1, keepdims=True) + EPS)
    return (y * g.astype(jnp.float32)).astype(x.dtype)


def _head_norm(o):
    return o * lax.rsqrt(jnp.mean(o * o, axis=-1, keepdims=True) + EPS)


def _rotary(x, pos):
    half = x.shape[-1] // 2
    inv = 1.0 / (ROPE_BASE ** jnp.linspace(0.0, 1.0, half, dtype=jnp.float32))
    ang = pos[:, None] * inv[None, :]
    cos = jnp.cos(ang)[None, :, None, :]
    sin = jnp.sin(ang)[None, :, None, :]
    x1, x2 = x[..., :half], x[..., half:]
    return jnp.concatenate([x1 * cos - x2 * sin, x1 * sin + x2 * cos], axis=-1)


def _causal_conv(x, buf, w, b):
    T = x.shape[1]
    xc = jnp.concatenate([buf.astype(x.dtype), x], axis=1)
    y = b.astype(x.dtype) + sum(xc[:, j:j + T] * w[j].astype(x.dtype) for j in range(CONV_W))
    return y, xc[:, xc.shape[1] - (CONV_W - 1):]


def _gated_linear_chunk(q, k, v, log_a, s0, chunk):
    B, T, H, dk = q.shape
    dv = v.shape[-1]
    n = T // chunk
    q = q.reshape(B, n, chunk, H, dk)
    k = k.reshape(B, n, chunk, H, dk)
    v = v.reshape(B, n, chunk, H, dv)
    b = jnp.cumsum(log_a.reshape(B, n, chunk, H, dk), axis=2)
    b_last = b[:, :, -1]
    q_in = q * jnp.exp(b)
    k_in = k * jnp.exp(-b)
    k_st = k * jnp.exp(b_last[:, :, None] - b)
    mask = jnp.tril(jnp.ones((chunk, chunk), dtype=bool))
    scores = jnp.where(mask, jnp.einsum('bnthc,bnshc->bnhts', q_in, k_in), 0.0)
    o = jnp.einsum('bnhts,bnshv->bnthv', scores, v)
    ds = jnp.einsum('bnshc,bnshv->bnhcv', k_st, v)

    def step(s, inp):
        dec, d = inp
        return dec[..., None] * s + d, s

    s_fin, s_prev = lax.scan(step, s0, (jnp.moveaxis(jnp.exp(b_last), 1, 0), jnp.moveaxis(ds, 1, 0)))
    s_prev = jnp.moveaxis(s_prev, 0, 1)
    o = o + jnp.einsum('bnthc,bnhcv->bnthv', q_in, s_prev)
    return o.reshape(B, T, H, dv), s_fin


def _mlstm_chunk(q, k, v, ig, lf, state, chunk):
    c0, n0, m0 = state
    B, T, H, d = q.shape
    n = T // chunk
    q, k, v = (a.reshape(B, n, chunk, H, d) for a in (q, k, v))
    ig = ig.reshape(B, n, chunk, H)
    b = jnp.cumsum(lf.reshape(B, n, chunk, H), axis=2)
    a = ig - b
    g = lax.cummax(a, axis=2)
    b_last = b[:, :, -1]
    m_loc = b_last + g[:, :, -1]
    w_st = jnp.exp(a + (b_last - m_loc)[:, :, None])
    dc = jnp.einsum('bnsh,bnshk,bnshv->bnhkv', w_st, k, v)
    dn = jnp.einsum('bnsh,bnshk->bnhk', w_st, k)

    def step(carry, inp):
        c, nn, m = carry
        bl, ml, dci, dni = inp
        m_new = jnp.maximum(bl + m, ml)
        s_old = jnp.exp(bl + m - m_new)
        s_new = jnp.exp(ml - m_new)
        c_new = s_old[..., None, None] * c + s_new[..., None, None] * dci
        n_new = s_old[..., None] * nn + s_new[..., None] * dni
        return (c_new, n_new, m_new), (c, nn, m)

    mv = lambda t: jnp.moveaxis(t, 1, 0)
    (c_f, n_f, m_f), (c_p, n_p, m_p) = lax.scan(step, (c0, n0, m0), (mv(b_last), mv(m_loc), mv(dc), mv(dn)))
    c_p, n_p, m_p = (jnp.moveaxis(t, 0, 1) for t in (c_p, n_p, m_p))
    m_t = b + jnp.maximum(m_p[:, :, None], g)
    w_inter = jnp.exp(b + m_p[:, :, None] - m_t)
    log_d = jnp.swapaxes(b - m_t, 2, 3)[..., :, None] + jnp.swapaxes(a, 2, 3)[..., None, :]
    mask = jnp.tril(jnp.ones((chunk, chunk), dtype=bool))
    dmat = jnp.exp(jnp.where(mask, log_d, -jnp.inf))
    scores = jnp.einsum('bnthd,bnshd->bnhts', q, k) * dmat
    num = (jnp.einsum('bnhts,bnshv->bnthv', scores, v)
           + w_inter[..., None] * jnp.einsum('bnthk,bnhkv->bnthv', q, c_p))
    den = jnp.swapaxes(scores.sum(-1), 2, 3) + w_inter * jnp.einsum('bnthk,bnhk->bnth', q, n_p)
    h = num / jnp.maximum(jnp.abs(den), jnp.exp(-m_t))[..., None]
    return h.reshape(B, T, H, d), (c_f, n_f, m_f)


def _over_segments(fn, arrays, state, segs):
    outs, start = [], 0
    for length in segs:
        part = [a[:, start:start + length] for a in arrays]
        o, state = fn(*part, state, min(CHUNK, length))
        outs.append(o)
        start += length
    return jnp.concatenate(outs, axis=1), state


def _layer(x, pos, segs, st, w):
    s_ret, c_m, n_m, m_m, conv_buf, s_gla = st
    (norm1, w_in, b_i, b_f, conv_w, conv_b, w_mq, w_mk, w_mv, m_skip, w_a2, b_a,
     w_br_ret, w_br_mlstm, w_br_gla, w_out, norm2, w_ffn_in, w_ffn_out) = w
    f32 = jnp.float32
    B, T, _ = x.shape
    h = _rmsnorm(x, norm1)
    offs = tuple(int(o) for o in np.cumsum(IN_WIDTHS)[:-1])
    (rq, rk, rv, rg, mx, mz, mi, mf, gq, gk, gv, gr, ga,
     z_ret, z_mlstm, z_gla) = jnp.split(h @ w_in, offs, axis=-1)

    log_gamma = jnp.log(1.0 - 2.0 ** (-5.0 - jnp.arange(R_HEADS, dtype=f32)))
    rq_h = _rotary(rq.astype(f32).reshape(B, T, R_HEADS, R_DK), pos)
    rk_h = _rotary(rk.astype(f32).reshape(B, T, R_HEADS, R_DK), pos) * (R_DK ** -0.5)
    rv_h = rv.astype(f32).reshape(B, T, R_HEADS, R_DV)
    la_r = jnp.broadcast_to(log_gamma[None, None, :, None], rq_h.shape)
    o_r, s_ret = _over_segments(_gated_linear_chunk, (rq_h, rk_h, rv_h, la_r), s_ret.astype(f32), segs)
    o_r = _head_norm(o_r).reshape(B, T, R_V) * jax.nn.silu(rg.astype(f32))
    p_ret = o_r.astype(x.dtype) @ w_br_ret

    c_pre, conv_new = _causal_conv(mx, conv_buf, conv_w, conv_b)
    c = jax.nn.silu(c_pre.astype(f32))
    c_h = c.reshape(B, T, M_HEADS, M_DH)
    mq = jnp.einsum('bthd,hde->bthe', c_h, w_mq.astype(f32))
    mk = jnp.einsum('bthd,hde->bthe', c_h, w_mk.astype(f32)) * (M_DH ** -0.5)
    mvv = jnp.einsum('bthd,hde->bthe', mx.astype(f32).reshape(B, T, M_HEADS, M_DH), w_mv.astype(f32))
    ig = mi.astype(f32) + b_i.astype(f32)
    lf = jax.nn.log_sigmoid(mf.astype(f32) + b_f.astype(f32))
    h_m, (c_m, n_m, m_m) = _over_segments(
        _mlstm_chunk, (mq, mk, mvv, ig, lf),
        (c_m.astype(f32), n_m.astype(f32), m_m.astype(f32)), segs)
    o_m = jax.nn.sigmoid(mz.astype(f32)) * (_head_norm(h_m).reshape(B, T, M_W) + m_skip.astype(f32) * c)
    p_mlstm = o_m.astype(x.dtype) @ w_br_mlstm

    gq_h = gq.astype(f32).reshape(B, T, G_HEADS, G_DK) * (G_DK ** -0.5)
    gk_h = gk.astype(f32).reshape(B, T, G_HEADS, G_DK)
    gv_h = gv.astype(f32).reshape(B, T, G_HEADS, G_DV)
    la_g = (jax.nn.log_sigmoid((ga @ w_a2 + b_a).astype(f32)) / G_NORMALIZER).reshape(B, T, G_HEADS, G_DK)
    o_g, s_gla = _over_segments(_gated_linear_chunk, (gq_h, gk_h, gv_h, la_g), s_gla.astype(f32), segs)
    o_g = _head_norm(o_g).reshape(B, T, G_V) * jax.nn.silu(gr.astype(f32))
    p_gla = o_g.astype(x.dtype) @ w_br_gla

    sg = lambda z: jax.nn.sigmoid(z.astype(f32)).astype(x.dtype)
    mix = sg(z_ret) * p_ret + sg(z_mlstm) * p_mlstm + sg(z_gla) * p_gla
    x = x + mix @ w_out

    h2 = _rmsnorm(x, norm2)
    u_g, u_v = jnp.split(h2 @ w_ffn_in, 2, axis=-1)
    x = x + (jax.nn.silu(u_g) * u_v) @ w_ffn_out
    return x, (s_ret, c_m, n_m, m_m, conv_new, s_gla)


def _trunk(x, pos, segs, states, weights, norm_f):
    new = []
    for l in range(DEPTH):
        x, st = _layer(x, pos, segs, tuple(s[l] for s in states), tuple(w[l] for w in weights))
        new.append(st)
    stacked = tuple(jnp.stack([st[i] for st in new]) for i in range(6))
    return _rmsnorm(x, norm_f), stacked


def setup_inputs(seed: int = 0) -> dict:
    key = jax.random.key(seed)
    ks = jax.random.split(key, 32)
    nrm = lambda k, shape, s: jax.random.normal(k, shape, jnp.float32) * s
    return {
        'x_prompt': nrm(ks[0], (BATCH, SEQ, D_MODEL), 1.0),
        'x_sample': nrm(ks[1], (DEC_BATCH, DEC_SEQ, D_MODEL), 1.0),
        'state_ret': nrm(ks[2], (DEPTH, DEC_BATCH, R_HEADS, R_DK, R_DV), 0.3),
        'state_mlstm_c': nrm(ks[3], (DEPTH, DEC_BATCH, M_HEADS, M_DH, M_DH), 0.3),
        'state_mlstm_n': nrm(ks[4], (DEPTH, DEC_BATCH, M_HEADS, M_DH), 0.3),
        'state_mlstm_m': 1.0 + nrm(ks[5], (DEPTH, DEC_BATCH, M_HEADS), 0.5),
        'state_mlstm_conv': nrm(ks[6], (DEPTH, DEC_BATCH, CONV_W - 1, M_W), 1.0),
        'state_gla': nrm(ks[7], (DEPTH, DEC_BATCH, G_HEADS, G_DK, G_DV), 0.3),
        'meta_tokens': nrm(ks[8], (N_META, D_MODEL), 1.0),
        'norm1': 1.0 + nrm(ks[9], (DEPTH, D_MODEL), 0.01),
        'w_in': nrm(ks[10], (DEPTH, D_MODEL, D_IN), D_MODEL ** -0.5),
        'b_mlstm_i': nrm(ks[11], (DEPTH, M_HEADS), 0.1),
        'b_mlstm_f': jnp.linspace(3.0, 6.0, M_HEADS, dtype=jnp.float32)[None] + nrm(ks[12], (DEPTH, M_HEADS), 0.1),
        'conv_w': nrm(ks[13], (DEPTH, CONV_W, M_W), CONV_W ** -0.5),
        'conv_b': nrm(ks[14], (DEPTH, M_W), 0.01),
        'w_mq': nrm(ks[15], (DEPTH, M_HEADS, M_DH, M_DH), M_DH ** -0.5),
        'w_mk': nrm(ks[16], (DEPTH, M_HEADS, M_DH, M_DH), M_DH ** -0.5),
        'w_mv': nrm(ks[17], (DEPTH, M_HEADS, M_DH, M_DH), M_DH ** -0.5),
        'm_skip': 1.0 + nrm(ks[18], (DEPTH, M_W), 0.1),
        'w_gla_a2': nrm(ks[19], (DEPTH, G_RANK, G_QK), G_RANK ** -0.5),
        'b_gla_a': nrm(ks[20], (DEPTH, G_QK), 0.1),
        'w_br_ret': nrm(ks[21], (DEPTH, R_V, D_MODEL), R_V ** -0.5),
        'w_br_mlstm': nrm(ks[22], (DEPTH, M_W, D_MODEL), M_W ** -0.5),
        'w_br_gla': nrm(ks[23], (DEPTH, G_V, D_MODEL), G_V ** -0.5),
        'w_out': nrm(ks[24], (DEPTH, D_MODEL, D_MODEL), D_MODEL ** -0.5),
        'norm2': 1.0 + nrm(ks[25], (DEPTH, D_MODEL), 0.01),
        'w_ffn_in': nrm(ks[26], (DEPTH, D_MODEL, 2 * D_FF), D_MODEL ** -0.5),
        'w_ffn_out': nrm(ks[27], (DEPTH, D_FF, D_MODEL), D_FF ** -0.5),
        'norm_f': 1.0 + nrm(ks[28], (D_MODEL,), 0.01),
    }


def reference(x_prompt, x_sample, state_ret, state_mlstm_c, state_mlstm_n, state_mlstm_m,
              state_mlstm_conv, state_gla, meta_tokens, norm1, w_in, b_mlstm_i, b_mlstm_f,
              conv_w, conv_b, w_mq, w_mk, w_mv, m_skip, w_gla_a2, b_gla_a, w_br_ret,
              w_br_mlstm, w_br_gla, w_out, norm2, w_ffn_in, w_ffn_out, norm_f):
    f32 = jnp.float32
    weights = (norm1, w_in, b_mlstm_i, b_mlstm_f, conv_w, conv_b, w_mq, w_mk, w_mv, m_skip,
               w_gla_a2, b_gla_a, w_br_ret, w_br_mlstm, w_br_gla, w_out, norm2, w_ffn_in, w_ffn_out)

    B, S, _ = x_prompt.shape
    meta = jnp.broadcast_to(meta_tokens.astype(x_prompt.dtype)[None], (B, N_META, D_MODEL))
    xp = jnp.concatenate([meta, x_prompt], axis=1)
    pos_p = jnp.arange(N_META + S, dtype=f32)
    zeros = (jnp.zeros((DEPTH, B, R_HEADS, R_DK, R_DV), f32),
             jnp.zeros((DEPTH, B, M_HEADS, M_DH, M_DH), f32),
             jnp.zeros((DEPTH, B, M_HEADS, M_DH), f32),
             jnp.zeros((DEPTH, B, M_HEADS), f32),
             jnp.zeros((DEPTH, B, CONV_W - 1, M_W), x_prompt.dtype),
             jnp.zeros((DEPTH, B, G_HEADS, G_DK, G_DV), f32))
    yp, (p_ret, p_c, p_n, p_m, p_conv, p_gla) = _trunk(xp, pos_p, (N_META, S), zeros, weights, norm_f)
    y_prompt = yp[:, N_META:]

    T = x_sample.shape[1]
    pos_s = (N_META + PAST_LEN) + jnp.arange(T, dtype=f32)
    y_sample, (s_ret, s_c, s_n, s_m, s_conv, s_gla) = _trunk(
        x_sample, pos_s, (T,),
        (state_ret, state_mlstm_c, state_mlstm_n, state_mlstm_m, state_mlstm_conv, state_gla),
        weights, norm_f)
    return (y_prompt, y_sample, p_ret, p_c, p_n, p_m, p_conv, p_gla, s_ret, s_c, s_n, s_m, s_conv, s_gla)
```

```python
import functools
import math

import jax
import jax.numpy as jnp
import numpy as np
from jax import lax
from jax.experimental import pallas as pl
from jax.experimental.pallas import tpu as pltpu

D_MODEL = 1024
DEPTH = 2
N_META = 16
PAST_LEN = 2048
EPS = 1e-6
R_HEADS, R_DK, R_DV = 4, 64, 128
ROPE_BASE = 10000.0
M_HEADS, M_DH, CONV_W = 4, 128, 4
G_HEADS, G_DK, G_DV, G_RANK = 4, 64, 128, 16
G_NORMALIZER = 16.0
R_QK, R_V = R_HEADS * R_DK, R_HEADS * R_DV
M_W = M_HEADS * M_DH
G_QK, G_V = G_HEADS * G_DK, G_HEADS * G_DV
D_FF = ((-(-8 * D_MODEL // 3)) + 255) // 256 * 256
IN_WIDTHS = (R_QK, R_QK, R_V, R_V, M_W, M_W, M_HEADS, M_HEADS,
             G_QK, G_QK, G_V, G_V, G_RANK, D_MODEL, D_MODEL, D_MODEL)

LANES = 128
SUBLANES = 8
CHUNK = LANES
MXU_DTYPE = jnp.bfloat16
VMEM_LIMIT_BYTES = 56 * 1024 * 1024

_RET0, _ML0, _SM0, _GLA0, _MIXW = 0, 1536, 2560, 2688, 4224
_FF_TILE = 256
_NT = (((1,), (1,)), ((), ()))


def _dot(a, b):
    return jnp.dot(a.astype(MXU_DTYPE), b.astype(MXU_DTYPE), preferred_element_type=jnp.float32)


def _dot_nt(a, b):
    return lax.dot_general(a.astype(MXU_DTYPE), b.astype(MXU_DTYPE), _NT,
                           preferred_element_type=jnp.float32)


def _rmsnorm(x, g):
    return x * lax.rsqrt(jnp.mean(x * x, axis=-1, keepdims=True) + EPS) * g


def _head_norm(o):
    return o * lax.rsqrt(jnp.mean(o * o, axis=-1, keepdims=True) + EPS)


def _sigmoid(x):
    return 1.0 / (1.0 + jnp.exp(-x))


def _silu(x):
    return x * _sigmoid(x)


def _log_sigmoid(x):
    return -(jnp.maximum(-x, 0.0) + jnp.log(1.0 + jnp.exp(-jnp.abs(x))))


def _scan_lanes(x, op, fill):
    lane = lax.broadcasted_iota(jnp.int32, x.shape, 1)
    s = 1
    while s < LANES:
        x = op(x, jnp.where(lane >= s, pltpu.roll(x, s, axis=1), fill))
        s *= 2
    return x


def _cumsum_rows(x, tri):
    hi = x.astype(MXU_DTYPE)
    r1 = x - hi.astype(jnp.float32)
    mid = r1.astype(MXU_DTYPE)
    lo = (r1 - mid.astype(jnp.float32)).astype(MXU_DTYPE)
    f = functools.partial(jnp.dot, preferred_element_type=jnp.float32)
    return f(tri, hi) + f(tri, mid) + f(tri, lo)


def _gated_linear_chunk(q, k, v, b, blast, dec, s_ref, head_masks, blk_mask, causal):
    q_in = q * jnp.exp(b)
    k_in = (k * jnp.exp(-b)).astype(MXU_DTYPE)
    k_st = k * jnp.exp(blast - b)
    s_prev = s_ref[0]
    v_b = v.astype(MXU_DTYPE)
    o_inter = _dot(q_in, s_prev)
    outs = []
    for h in range(4):
        sc = _dot_nt(jnp.where(head_masks[h], q_in, 0.0), k_in)
        sc = jnp.where(causal, sc, 0.0)
        outs.append(_dot(sc, v_b[:, h * LANES:(h + 1) * LANES]) + o_inter[:, h * LANES:(h + 1) * LANES])
    ds = _dot(k_st.T, v_b)
    s_ref[0] = dec * s_prev + jnp.where(blk_mask, ds, 0.0)
    return outs


def _mixer_kernel(nreal, tb,
                  x_ref, cos_ref, sin_ref, sret0, cext0, m0, conv0, sgla0,
                  g1_ref, wmix_ref, wa2_ref, ba_ref, gb_ref, cw_ref, cb_ref,
                  wq_ref, wk_ref, wv_ref, skip_ref,
                  o_ref, sret_o, cext_o, m_o, conv_o, sgla_o,
                  zret_s, zml_s, zgla_s, xc_s, c_s, q_s, k_s, v_s, la_s, ig_s, lf_s):
    f32 = jnp.float32
    nc = tb // CHUNK
    nreal = min(nreal, tb)
    masked = nreal < tb

    @pl.when(pl.program_id(1) == 0)
    def _():
        sret_o[...] = sret0[...]
        cext_o[...] = cext0[...]
        m_o[...] = m0[...]
        conv_o[...] = conv0[...]
        sgla_o[...] = sgla0[...]

    h = _rmsnorm(x_ref[0], g1_ref[...])
    if masked:
        h = jnp.where(lax.broadcasted_iota(jnp.int32, h.shape, 0) < nreal, h, 0.0)
    h = h.astype(MXU_DTYPE)
    zret_s[...] = jnp.dot(h, wmix_ref[:, _RET0:_ML0], preferred_element_type=f32)
    zml_s[...] = jnp.dot(h, wmix_ref[:, _ML0:_SM0], preferred_element_type=f32)
    zs = jnp.dot(h, wmix_ref[:, _SM0:_GLA0], preferred_element_type=f32)
    zgla_s[...] = jnp.dot(h, wmix_ref[:, _GLA0:_MIXW], preferred_element_type=f32)

    cos = cos_ref[...]
    sin = sin_ref[...]
    for c0, scale in ((0, 1.0), (R_QK, R_DK ** -0.5)):
        x1 = zret_s[:, c0:c0 + LANES]
        x2 = zret_s[:, c0 + LANES:c0 + 2 * LANES]
        zret_s[:, c0:c0 + LANES] = (x1 * cos - x2 * sin) * scale
        zret_s[:, c0 + LANES:c0 + 2 * LANES] = (x1 * sin + x2 * cos) * scale

    xc_s[0:SUBLANES, :] = conv_o[0]
    xc_s[SUBLANES:SUBLANES + tb, :] = zml_s[:, 0:M_W]
    cpre = cb_ref[...] + sum(
        xc_s[SUBLANES - (CONV_W - 1) + j:SUBLANES - (CONV_W - 1) + j + tb, :] * cw_ref[j:j + 1, :]
        for j in range(CONV_W))
    conv_o[0] = xc_s[nreal:nreal + SUBLANES, :]
    c_act = _silu(cpre)
    c_s[...] = c_act
    for hd in range(M_HEADS):
        sl = slice(hd * M_DH, (hd + 1) * M_DH)
        ch = c_act[:, sl].astype(MXU_DTYPE)
        q_s[:, sl] = jnp.dot(ch, wq_ref[hd], preferred_element_type=f32)
        k_s[:, sl] = jnp.dot(ch, wk_ref[hd], preferred_element_type=f32) * (M_DH ** -0.5)
        v_s[:, sl] = jnp.dot(zml_s[:, sl].astype(MXU_DTYPE), wv_ref[hd], preferred_element_type=f32)

    zst = zs.T
    ig_all = zst[0:SUBLANES, :] + gb_ref[0:SUBLANES, 0:1]
    lf_all = _log_sigmoid(zst[SUBLANES:2 * SUBLANES, :] + gb_ref[SUBLANES:2 * SUBLANES, 0:1])
    if masked:
        tok = lax.broadcasted_iota(jnp.int32, ig_all.shape, 1)
        ig_all = jnp.where(tok < nreal, ig_all, -jnp.inf)
        lf_all = jnp.where(tok < nreal, lf_all, 0.0)
    for c in range(nc):
        ig_s[c] = ig_all[:, c * CHUNK:(c + 1) * CHUNK]
        lf_s[c] = lf_all[:, c * CHUNK:(c + 1) * CHUNK]

    la = _log_sigmoid(_dot(zs, wa2_ref[...]) + ba_ref[...]) * (1.0 / G_NORMALIZER)
    if masked:
        la = jnp.where(lax.broadcasted_iota(jnp.int32, la.shape, 0) < nreal, la, 0.0)
    la_s[...] = la

    row = lax.broadcasted_iota(jnp.int32, (CHUNK, CHUNK), 0)
    col = lax.broadcasted_iota(jnp.int32, (CHUNK, CHUNK), 1)
    causal = row >= col
    tri = causal.astype(MXU_DTYPE)
    lane_q = lax.broadcasted_iota(jnp.int32, (1, R_QK), 1)
    ret_head = (lane_q % LANES) // (R_DK // 2)
    gla_head = lane_q // G_DK
    ret_masks = [ret_head == hd for hd in range(R_HEADS)]
    gla_masks = [gla_head == hd for hd in range(G_HEADS)]
    srow = lax.broadcasted_iota(jnp.int32, (R_QK, R_V), 0)
    scol = lax.broadcasted_iota(jnp.int32, (R_QK, R_V), 1) // R_DV
    ret_blk = ((srow % LANES) // (R_DK // 2)) == scol
    gla_blk = (srow // G_DK) == scol

    def log_gamma(head):
        lg = jnp.full(head.shape, math.log(1.0 - 2.0 ** -5.0), f32)
        for hd in range(1, R_HEADS):
            lg = jnp.where(head == hd, math.log(1.0 - 2.0 ** (-5.0 - hd)), lg)
        return lg

    cnt = jnp.minimum(lax.broadcasted_iota(jnp.int32, (CHUNK, R_QK), 0) + 1, nreal).astype(f32)
    n_last = float(min(CHUNK, nreal))
    b_ret = cnt * log_gamma(ret_head)
    blast_ret = n_last * log_gamma(ret_head)
    dec_ret = jnp.exp(n_last * log_gamma((srow % LANES) // (R_DK // 2)))
    ones_col = (lax.broadcasted_iota(jnp.int32, (CHUNK, LANES), 1) == 0).astype(f32)

    def chunk_body(c, carry):
        rows = pl.ds(pl.multiple_of(c * CHUNK, CHUNK), CHUNK)

        outs = _gated_linear_chunk(zret_s[rows, 0:R_QK], zret_s[rows, R_QK:2 * R_QK],
                                   zret_s[rows, 2 * R_QK:2 * R_QK + R_V],
                                   b_ret, blast_ret, dec_ret, sret_o, ret_masks, ret_blk, causal)
        for hd in range(R_HEADS):
            gate = _silu(zret_s[rows, 2 * R_QK + R_V + hd * R_DV:2 * R_QK + R_V + (hd + 1) * R_DV])
            o_ref[0, rows, hd * R_DV:(hd + 1) * R_DV] = (_head_norm(outs[hd]) * gate).astype(o_ref.dtype)

        b_g = _cumsum_rows(la_s[rows, :], tri)
        blast_g = b_g[CHUNK - 1:CHUNK, :]
        dec_col = jnp.exp(jnp.broadcast_to(blast_g, (LANES, G_QK))).T
        dec_g = jnp.concatenate([dec_col] * G_HEADS, axis=1)
        outs = _gated_linear_chunk(zgla_s[rows, 0:G_QK] * (G_DK ** -0.5), zgla_s[rows, G_QK:2 * G_QK],
                                   zgla_s[rows, 2 * G_QK:2 * G_QK + G_V],
                                   b_g, blast_g, dec_g, sgla_o, gla_masks, gla_blk, causal)
        for hd in range(G_HEADS):
            gate = _silu(zgla_s[rows, 2 * G_QK + G_V + hd * G_DV:2 * G_QK + G_V + (hd + 1) * G_DV])
            o_ref[0, rows, R_V + M_W + hd * G_DV:R_V + M_W + (hd + 1) * G_DV] = (
                _head_norm(outs[hd]) * gate).astype(o_ref.dtype)

        ig = ig_s[c]
        lf = lf_s[c]
        b = _scan_lanes(lf, jnp.add, 0.0)
        blast = jnp.sum(lf, axis=1, keepdims=True)
        a = ig - b
        g = _scan_lanes(a, jnp.maximum, -jnp.inf)
        mloc = blast + jnp.max(a, axis=1, keepdims=True)
        m_p = m_o[0][:, 0:1]
        m_new = jnp.maximum(blast + m_p, mloc)
        s_old = jnp.broadcast_to(jnp.exp(blast + m_p - m_new), (SUBLANES, LANES))
        s_new = jnp.broadcast_to(jnp.exp(mloc - m_new), (SUBLANES, LANES))
        w_st = jnp.exp(a + (blast - mloc))
        m_t = b + jnp.maximum(m_p, g)
        w_inter = jnp.exp(b + m_p - m_t)
        stats = jnp.concatenate(
            [w_st, b - m_t, w_inter, jnp.exp(-m_t), jnp.zeros((LANES - 4 * SUBLANES, LANES), f32)], axis=0)
        cols = stats.T
        m_o[0] = jnp.broadcast_to(m_new, (SUBLANES, LANES))

        for hd in range(M_HEADS):
            sl = slice(hd * M_DH, (hd + 1) * M_DH)
            qh = q_s[rows, sl].astype(MXU_DTYPE)
            kh = k_s[rows, sl]
            vh = v_s[rows, sl]
            wst_c = cols[:, hd:hd + 1]
            rt_c = cols[:, SUBLANES + hd:SUBLANES + hd + 1]
            wi_c = cols[:, 2 * SUBLANES + hd:2 * SUBLANES + hd + 1]
            ei_c = cols[:, 3 * SUBLANES + hd:3 * SUBLANES + hd + 1]
            dmat = jnp.exp(jnp.where(causal, rt_c + a[hd:hd + 1, :], -jnp.inf))
            sc = _dot_nt(qh, kh) * dmat
            cext = cext_o[0, hd]
            inter = _dot(qh, cext)
            num = _dot(sc, vh) + wi_c * inter[:, 0:M_DH]
            den = jnp.sum(sc, axis=1, keepdims=True) + wi_c * inter[:, M_DH:M_DH + 1]
            hh = num / jnp.maximum(jnp.abs(den), ei_c)
            vext = jnp.concatenate([vh, ones_col], axis=1)
            dcext = _dot((kh * wst_c).T, vext)
            so = jnp.concatenate([s_old[hd:hd + 1, :]] * 2, axis=1)
            sn = jnp.concatenate([s_new[hd:hd + 1, :]] * 2, axis=1)
            cext_o[0, hd] = so * cext + sn * dcext
            o_m = _sigmoid(zml_s[rows, M_W + hd * M_DH:M_W + (hd + 1) * M_DH]) * (
                _head_norm(hh) + skip_ref[:, sl] * c_s[rows, sl])
            o_ref[0, rows, R_V + hd * M_DH:R_V + (hd + 1) * M_DH] = o_m.astype(o_ref.dtype)
        return carry

    lax.fori_loop(0, nc, chunk_body, 0)


def _mixer_call(x, cos, sin, init, w, *, nreal, tb):
    bsz, t, _ = x.shape
    nt = t // tb
    nc = tb // CHUNK
    sret0, cext0, m0, conv0, sgla0 = init

    def state_spec(arr):
        nd = arr.ndim
        if arr.shape[0] == 1:
            return pl.BlockSpec((1,) + arr.shape[1:], lambda b, i: (0,) * nd)
        return pl.BlockSpec((1,) + arr.shape[1:], lambda b, i: (b,) + (0,) * (nd - 1))

    def full_spec(arr):
        nd = arr.ndim
        return pl.BlockSpec(arr.shape, lambda b, i: (0,) * nd)

    def out_state(shape):
        nd = len(shape) + 1
        return (jax.ShapeDtypeStruct((bsz,) + shape, jnp.float32),
                pl.BlockSpec((1,) + shape, lambda b, i: (b,) + (0,) * (nd - 1)))

    weights = (w["g1"], w["wmix"], w["wa2"], w["ba"], w["gate_b"], w["conv_w"], w["conv_b"],
               w["wq"], w["wk"], w["wv"], w["skip"])
    outs = [(jax.ShapeDtypeStruct((bsz, t, R_V + M_W + G_V), MXU_DTYPE),
             pl.BlockSpec((1, tb, R_V + M_W + G_V), lambda b, i: (b, i, 0))),
            out_state((R_QK, R_V)), out_state((M_HEADS, M_DH, 2 * M_DH)), out_state((SUBLANES, LANES)),
            out_state((SUBLANES, M_W)), out_state((G_QK, G_V))]
    f32 = jnp.float32
    scratch = [pltpu.VMEM((tb, 2 * R_QK + 2 * R_V), f32), pltpu.VMEM((tb, 2 * M_W), f32),
               pltpu.VMEM((tb, 2 * G_QK + 2 * G_V), f32), pltpu.VMEM((tb + SUBLANES, M_W), f32),
               pltpu.VMEM((tb, M_W), f32), pltpu.VMEM((tb, M_W), f32), pltpu.VMEM((tb, M_W), f32),
               pltpu.VMEM((tb, M_W), f32), pltpu.VMEM((tb, G_QK), f32),
               pltpu.VMEM((nc, SUBLANES, LANES), f32), pltpu.VMEM((nc, SUBLANES, LANES), f32)]
    return pl.pallas_call(
        functools.partial(_mixer_kernel, nreal, tb),
        grid=(bsz, nt),
        in_specs=[pl.BlockSpec((1, tb, D_MODEL), lambda b, i: (b, i, 0)),
                  pl.BlockSpec((tb, LANES), lambda b, i: (i, 0)),
                  pl.BlockSpec((tb, LANES), lambda b, i: (i, 0)),
                  state_spec(sret0), state_spec(cext0), state_spec(m0), state_spec(conv0), state_spec(sgla0)]
                 + [full_spec(a) for a in weights],
        out_specs=[o[1] for o in outs],
        out_shape=[o[0] for o in outs],
        scratch_shapes=scratch,
        compiler_params=pltpu.CompilerParams(
            dimension_semantics=("arbitrary", "arbitrary"), vmem_limit_bytes=VMEM_LIMIT_BYTES),
        name="mixer",
    )(x, cos, sin, sret0, cext0, m0, conv0, sgla0, *weights)


def _channel_kernel(final, x_ref, o_ref, g1_ref, wg_ref, wbr_ref, wout_ref, g2_ref, wfi_ref, wfo_ref,
                    gf_ref, y_ref):
    f32 = jnp.float32
    x = x_ref[...]
    h = _rmsnorm(x, g1_ref[...]).astype(MXU_DTYPE)
    mix = None
    for br, (c0, width) in enumerate(((0, R_V), (R_V, M_W), (R_V + M_W, G_V))):
        gate = _sigmoid(jnp.dot(h, wg_ref[:, br * D_MODEL:(br + 1) * D_MODEL], preferred_element_type=f32))
        p = jnp.dot(o_ref[:, c0:c0 + width], wbr_ref[c0:c0 + width, :], preferred_element_type=f32)
        mix = gate * p if mix is None else mix + gate * p
    x = x + _dot(mix, wout_ref[...])
    h2 = _rmsnorm(x, g2_ref[...]).astype(MXU_DTYPE)
    acc = x
    for j in range(D_FF // _FF_TILE):
        cs = slice(j * _FF_TILE, (j + 1) * _FF_TILE)
        ug = jnp.dot(h2, wfi_ref[:, cs], preferred_element_type=f32)
        uv = jnp.dot(h2, wfi_ref[:, D_FF + j * _FF_TILE:D_FF + (j + 1) * _FF_TILE], preferred_element_type=f32)
        acc = acc + _dot(_silu(ug) * uv, wfo_ref[cs, :])
    y_ref[...] = _rmsnorm(acc, gf_ref[...]) if final else acc


def _channel_call(x2d, o2d, w, norm_f, *, final, tm):
    n = x2d.shape[0]
    weights = (w["g1"], w["wg"], w["wbr"], w["wout"], w["g2"], w["wfi"], w["wfo"], norm_f)

    def full_spec(arr):
        nd = arr.ndim
        return pl.BlockSpec(arr.shape, lambda i: (0,) * nd, pipeline_mode=pl.Buffered(1))

    return pl.pallas_call(
        functools.partial(_channel_kernel, final),
        grid=(n // tm,),
        in_specs=[pl.BlockSpec((tm, D_MODEL), lambda i: (i, 0)),
                  pl.BlockSpec((tm, o2d.shape[1]), lambda i: (i, 0))] + [full_spec(a) for a in weights],
        out_specs=pl.BlockSpec((tm, D_MODEL), lambda i: (i, 0)),
        out_shape=jax.ShapeDtypeStruct((n, D_MODEL), jnp.float32),
        compiler_params=pltpu.CompilerParams(
            dimension_semantics=("arbitrary",), vmem_limit_bytes=VMEM_LIMIT_BYTES),
        name="channel",
    )(x2d, o2d, *weights)


def _ret_perm():
    half, hd, i = np.meshgrid(np.arange(2), np.arange(R_HEADS), np.arange(R_DK // 2), indexing="ij")
    return (hd * R_DK + half * (R_DK // 2) + i).reshape(-1)


def _prep_layer(l, norm1, w_in, b_i, b_f, conv_w, conv_b, w_mq, w_mk, w_mv, m_skip, w_a2, b_a,
                w_br_ret, w_br_mlstm, w_br_gla, w_out, norm2, w_ffn_in, w_ffn_out):
    f32 = jnp.float32
    offs = np.cumsum((0,) + IN_WIDTHS)
    cols = [w_in[l][:, offs[i]:offs[i + 1]] for i in range(len(IN_WIDTHS))]
    rq, rk, rv, rg, mx, mz, mi, mf, gq, gk, gv, gr, ga, z_ret, z_ml, z_gla = cols
    perm = _ret_perm()
    small = jnp.zeros((D_MODEL, LANES), f32)
    small = small.at[:, 0:M_HEADS].set(mi).at[:, SUBLANES:SUBLANES + M_HEADS].set(mf)
    small = small.at[:, 2 * SUBLANES:2 * SUBLANES + G_RANK].set(ga)
    wmix = jnp.concatenate([rq[:, perm], rk[:, perm], rv, rg, mx, mz, small, gq, gk, gv, gr], axis=1)
    wa2 = jnp.zeros((LANES, G_QK), f32).at[2 * SUBLANES:2 * SUBLANES + G_RANK].set(w_a2[l])
    gate_b = jnp.zeros((2 * SUBLANES, LANES), f32)
    gate_b = gate_b.at[0:M_HEADS].set(jnp.broadcast_to(b_i[l][:, None], (M_HEADS, LANES)))
    gate_b = gate_b.at[SUBLANES:SUBLANES + M_HEADS].set(jnp.broadcast_to(b_f[l][:, None], (M_HEADS, LANES)))
    bf = MXU_DTYPE
    return dict(
        g1=norm1[l][None], wmix=wmix.astype(bf), wa2=wa2.astype(bf), ba=b_a[l][None], gate_b=gate_b,
        conv_w=conv_w[l], conv_b=conv_b[l][None], wq=w_mq[l].astype(bf), wk=w_mk[l].astype(bf),
        wv=w_mv[l].astype(bf), skip=m_skip[l][None],
        wg=jnp.concatenate([z_ret, z_ml, z_gla], axis=1).astype(bf),
        wbr=jnp.concatenate([w_br_ret[l], w_br_mlstm[l], w_br_gla[l]], axis=0).astype(bf),
        wout=w_out[l].astype(bf), g2=norm2[l][None], wfi=w_ffn_in[l].astype(bf), wfo=w_ffn_out[l].astype(bf))


def _rotary_tables(pos):
    half = R_DK // 2
    inv = 1.0 / (ROPE_BASE ** jnp.linspace(0.0, 1.0, half, dtype=jnp.float32))
    ang = pos[:, None] * inv[None, :]
    return jnp.tile(jnp.cos(ang), (1, R_HEADS)), jnp.tile(jnp.sin(ang), (1, R_HEADS))


def _pack_states(s_ret, c_m, n_m, m_m, conv, s_gla):
    f32 = jnp.float32
    bsz = s_ret.shape[0]
    half = R_DK // 2
    sret = jnp.zeros((bsz, 2, R_HEADS, half, R_HEADS, R_DV), f32)
    sgla = jnp.zeros((bsz, G_HEADS, G_DK, G_HEADS, G_DV), f32)
    for hd in range(R_HEADS):
        sret = sret.at[:, :, hd, :, hd, :].set(s_ret[:, hd].astype(f32).reshape(bsz, 2, half, R_DV))
        sgla = sgla.at[:, hd, :, hd, :].set(s_gla[:, hd].astype(f32))
    cext = jnp.concatenate([c_m.astype(f32), n_m.astype(f32)[..., None],
                            jnp.zeros((bsz, M_HEADS, M_DH, M_DH - 1), f32)], axis=-1)
    m = jnp.zeros((bsz, SUBLANES, LANES), f32).at[:, 0:M_HEADS, :].set(
        jnp.broadcast_to(m_m.astype(f32)[..., None], (bsz, M_HEADS, LANES)))
    cv = jnp.zeros((bsz, SUBLANES, M_W), f32).at[:, SUBLANES - (CONV_W - 1):, :].set(conv.astype(f32))
    return (sret.reshape(bsz, R_QK, R_V), cext, m, cv, sgla.reshape(bsz, G_QK, G_V))


def _unpack_states(sret, cext, m, cv, sgla):
    bsz = sret.shape[0]
    half = R_DK // 2
    r6 = sret.reshape(bsz, 2, R_HEADS, half, R_HEADS, R_DV)
    g5 = sgla.reshape(bsz, G_HEADS, G_DK, G_HEADS, G_DV)
    s_ret = jnp.stack([r6[:, :, hd, :, hd, :].reshape(bsz, R_DK, R_DV) for hd in range(R_HEADS)], axis=1)
    s_gla = jnp.stack([g5[:, hd, :, hd, :] for hd in range(G_HEADS)], axis=1)
    return (s_ret, cext[..., 0:M_DH], cext[..., M_DH], m[:, 0:M_HEADS, 0],
            cv[:, SUBLANES - (CONV_W - 1):, :], s_gla)


def _trunk(x, pos0, nreal, init_layers, layers, norm_f, *, tb, tm):
    bsz, t, _ = x.shape
    cos, sin = _rotary_tables(pos0 + jnp.arange(t, dtype=jnp.float32))
    states = []
    for l in range(DEPTH):
        res = _mixer_call(x, cos, sin, init_layers[l], layers[l], nreal=nreal, tb=tb)
        states.append(tuple(res[1:]))
        x = _channel_call(x.reshape(bsz * t, D_MODEL), res[0].reshape(bsz * t, -1), layers[l], norm_f,
                          final=(l == DEPTH - 1), tm=tm).reshape(bsz, t, D_MODEL)
    return x, states


def kernel(x_prompt, x_sample, state_ret, state_mlstm_c, state_mlstm_n, state_mlstm_m, state_mlstm_conv, state_gla, meta_tokens, norm1, w_in, b_mlstm_i, b_mlstm_f, conv_w, conv_b, w_mq, w_mk, w_mv, m_skip, w_gla_a2, b_gla_a, w_br_ret, w_br_mlstm, w_br_gla, w_out, norm2, w_ffn_in, w_ffn_out, norm_f):
    f32 = jnp.float32
    layers = [_prep_layer(l, norm1, w_in, b_mlstm_i, b_mlstm_f, conv_w, conv_b, w_mq, w_mk, w_mv, m_skip,
                          w_gla_a2, b_gla_a, w_br_ret, w_br_mlstm, w_br_gla, w_out, norm2, w_ffn_in,
                          w_ffn_out) for l in range(DEPTH)]
    gf = norm_f[None]
    bsz, seq, _ = x_prompt.shape
    dbs, dseq, _ = x_sample.shape

    zero = _pack_states(jnp.zeros((1, R_HEADS, R_DK, R_DV), f32), jnp.zeros((1, M_HEADS, M_DH, M_DH), f32),
                        jnp.zeros((1, M_HEADS, M_DH), f32), jnp.zeros((1, M_HEADS), f32),
                        jnp.zeros((1, CONV_W - 1, M_W), f32), jnp.zeros((1, G_HEADS, G_DK, G_DV), f32))
    xm = jnp.zeros((1, CHUNK, D_MODEL), f32).at[0, 0:N_META].set(meta_tokens.astype(f32))
    _, meta_states = _trunk(xm, 0.0, N_META, [zero] * DEPTH, layers, gf, tb=CHUNK, tm=CHUNK)

    yp, p_states = _trunk(x_prompt, float(N_META), 512, meta_states, layers, gf, tb=512, tm=512)

    xs = jnp.zeros((dbs, CHUNK, D_MODEL), f32).at[:, 0:dseq].set(x_sample)
    s_init = [_pack_states(state_ret[l], state_mlstm_c[l], state_mlstm_n[l], state_mlstm_m[l],
                           state_mlstm_conv[l], state_gla[l]) for l in range(DEPTH)]
    ys, s_states = _trunk(xs, float(N_META + PAST_LEN), dseq, s_init, layers, gf, tb=CHUNK, tm=512)

    def stack(states):
        per_layer = [_unpack_states(*st) for st in states]
        return tuple(jnp.stack([pl_[i] for pl_ in per_layer]) for i in range(6))

    return (yp, ys[:, 0:dseq]) + stack(p_states) + stack(s_states)
```

```python
import functools
import math

import jax
import jax.numpy as jnp
import numpy as np
from jax import lax
from jax.experimental import pallas as pl
from jax.experimental.pallas import tpu as pltpu

D_MODEL = 1024
DEPTH = 2
N_META = 16
PAST_LEN = 2048
EPS = 1e-6
R_HEADS, R_DK, R_DV = 4, 64, 128
ROPE_BASE = 10000.0
M_HEADS, M_DH, CONV_W = 4, 128, 4
G_HEADS, G_DK, G_DV, G_RANK = 4, 64, 128, 16
G_NORMALIZER = 16.0
R_QK, R_V = R_HEADS * R_DK, R_HEADS * R_DV
M_W = M_HEADS * M_DH
G_QK, G_V = G_HEADS * G_DK, G_HEADS * G_DV
D_FF = ((-(-8 * D_MODEL // 3)) + 255) // 256 * 256
IN_WIDTHS = (R_QK, R_QK, R_V, R_V, M_W, M_W, M_HEADS, M_HEADS,
             G_QK, G_QK, G_V, G_V, G_RANK, D_MODEL, D_MODEL, D_MODEL)

LANES = 128
SUBLANES = 8
CHUNK = LANES
MXU_DTYPE = jnp.bfloat16
VMEM_LIMIT_BYTES = 56 * 1024 * 1024

_RET0, _ML0, _SM0, _GLA0, _MIXW = 0, 1536, 2560, 2688, 4224
_FF_TILE = 256


def _dot(a, b):
    return jnp.dot(a.astype(MXU_DTYPE), b.astype(MXU_DTYPE), preferred_element_type=jnp.float32)


def _rmsnorm(x, g):
    return x * lax.rsqrt(jnp.mean(x * x, axis=-1, keepdims=True) + EPS) * g


def _head_norm(o):
    return o * lax.rsqrt(jnp.mean(o * o, axis=-1, keepdims=True) + EPS)


def _sigmoid(x):
    return 1.0 / (1.0 + jnp.exp(-x))


def _silu(x):
    return x * _sigmoid(x)


def _log_sigmoid(x):
    return -(jnp.maximum(-x, 0.0) + jnp.log(1.0 + jnp.exp(-jnp.abs(x))))


def _cumsum_rows(x, tri):
    hi = x.astype(MXU_DTYPE)
    r1 = x - hi.astype(jnp.float32)
    mid = r1.astype(MXU_DTYPE)
    lo = (r1 - mid.astype(jnp.float32)).astype(MXU_DTYPE)
    f = functools.partial(jnp.dot, preferred_element_type=jnp.float32)
    return f(tri, hi) + f(tri, mid) + f(tri, lo)


def _block_diag(v0, v1):
    z = jnp.zeros_like(v0)
    return jnp.concatenate([jnp.concatenate([v0, z], axis=1), jnp.concatenate([z, v1], axis=1)], axis=0)


def _linear_attn_chunk(c, rows, q_s, km_s, kst_s, v_s, vbd_s, dec, s_ref, blk_mask, causal2):
    q_in = q_s[rows, :]
    s_prev = s_ref[0]
    o_inter = jnp.dot(q_in, s_prev.astype(MXU_DTYPE), preferred_element_type=jnp.float32)
    outs = []
    for p in range(2):
        kt = jnp.concatenate([km_s[c * 4 + 2 * p], km_s[c * 4 + 2 * p + 1]], axis=1)
        sc = jnp.dot(q_in, kt, preferred_element_type=jnp.float32)
        sc = jnp.where(causal2, sc, 0.0).astype(MXU_DTYPE)
        outs.append(jnp.dot(sc, vbd_s[c * 2 + p], preferred_element_type=jnp.float32)
                    + o_inter[:, p * 2 * LANES:(p + 1) * 2 * LANES])
    ds = jnp.dot(kst_s[c], v_s[rows, :], preferred_element_type=jnp.float32)
    s_ref[0] = dec * s_prev + jnp.where(blk_mask, ds, 0.0)
    return outs


def _mixer_kernel(nreal, tb,
                  x_ref, cos_ref, sin_ref, sret0, cext0, m0, conv0, sgla0,
                  g1_ref, wmix_ref, wa2_ref, ba_ref, gb_ref, cw_ref, cb_ref,
                  wq_ref, wk_ref, wv_ref, skip_ref,
                  o_ref, sret_o, cext_o, m_o, conv_o, sgla_o,
                  zret_s, zml_s, zgla_s, xc_s, c_s,
                  rq_s, rkm_s, rkst_s, rv_s, rvbd_s,
                  gq_s, gkm_s, gkst_s, gv_s, gvbd_s, gdec_s,
                  mq_s, mqw_s, mkt_s, mkw_s, mv_s, dmat_s, cols_s, sold_s, snew_s):
    f32 = jnp.float32
    bf = MXU_DTYPE
    nc = tb // CHUNK
    nreal = min(nreal, tb)
    masked = nreal < tb

    @pl.when(pl.program_id(1) == 0)
    def _():
        sret_o[...] = sret0[...]
        cext_o[...] = cext0[...]
        m_o[...] = m0[...]
        conv_o[...] = conv0[...]
        sgla_o[...] = sgla0[...]

    h = _rmsnorm(x_ref[0], g1_ref[...])
    if masked:
        h = jnp.where(lax.broadcasted_iota(jnp.int32, h.shape, 0) < nreal, h, 0.0)
    h = h.astype(bf)
    zs = jnp.dot(h, wmix_ref[:, _SM0:_GLA0], preferred_element_type=f32)
    zml_s[...] = jnp.dot(h, wmix_ref[:, _ML0:_SM0], preferred_element_type=f32)
    zret_s[...] = jnp.dot(h, wmix_ref[:, _RET0:_ML0], preferred_element_type=f32)
    zgla_s[...] = jnp.dot(h, wmix_ref[:, _GLA0:_MIXW], preferred_element_type=f32)

    row = lax.broadcasted_iota(jnp.int32, (CHUNK, CHUNK), 0)
    col = lax.broadcasted_iota(jnp.int32, (CHUNK, CHUNK), 1)
    causal = row >= col
    causal2 = jnp.concatenate([causal, causal], axis=1)
    tri = causal.astype(bf)

    zst = zs.T
    ig_all = zst[0:SUBLANES, :] + gb_ref[0:SUBLANES, 0:1]
    lf_all = _log_sigmoid(zst[SUBLANES:2 * SUBLANES, :] + gb_ref[SUBLANES:2 * SUBLANES, 0:1])
    tok = lax.broadcasted_iota(jnp.int32, ig_all.shape, 1)
    if masked:
        ig_all = jnp.where(tok < nreal, ig_all, -jnp.inf)
        lf_all = jnp.where(tok < nreal, lf_all, 0.0)
    in_chunk = tok % CHUNK

    def seg_scan(x, op, fill):
        s = 1
        while s < CHUNK:
            x = op(x, jnp.where(in_chunk >= s, pltpu.roll(x, s, axis=1), fill))
            s *= 2
        return x

    b_all = seg_scan(lf_all, jnp.add, 0.0)
    a_all = ig_all - b_all
    g_all = seg_scan(a_all, jnp.maximum, -jnp.inf)
    m_p = m_o[0][:, 0:1]
    w_st, wi_cols = [], []
    for c in range(nc):
        ls = slice(c * CHUNK, (c + 1) * CHUNK)
        b_c, a_c, g_c = b_all[:, ls], a_all[:, ls], g_all[:, ls]
        blast = jnp.sum(lf_all[:, ls], axis=1, keepdims=True)
        mloc = blast + jnp.max(a_c, axis=1, keepdims=True)
        m_new = jnp.maximum(blast + m_p, mloc)
        sold_s[c] = jnp.broadcast_to(jnp.exp(blast + m_p - m_new), (SUBLANES, LANES))
        snew_s[c] = jnp.broadcast_to(jnp.exp(mloc - m_new), (SUBLANES, LANES))
        w_st.append(jnp.exp(a_c + (blast - mloc)))
        m_t = b_c + jnp.maximum(m_p, g_c)
        stats = jnp.concatenate([b_c - m_t, jnp.exp(b_c + m_p - m_t), jnp.exp(-m_t),
                                 jnp.zeros((LANES - 3 * SUBLANES, LANES), f32)], axis=0)
        cols = stats.T
        cols_s[c] = cols
        wi_cols.append(cols)
        for hd in range(M_HEADS):
            dmat_s[c * M_HEADS + hd] = jnp.exp(
                jnp.where(causal, cols[:, hd:hd + 1] + a_c[hd:hd + 1, :], -jnp.inf))
        m_p = m_new
    m_o[0] = jnp.broadcast_to(m_p, (SUBLANES, LANES))

    xc_s[0:SUBLANES, :] = conv_o[0]
    xc_s[SUBLANES:SUBLANES + tb, :] = zml_s[:, 0:M_W]
    cpre = cb_ref[...] + sum(
        xc_s[SUBLANES - (CONV_W - 1) + j:SUBLANES - (CONV_W - 1) + j + tb, :] * cw_ref[j:j + 1, :]
        for j in range(CONV_W))
    conv_o[0] = xc_s[nreal:nreal + SUBLANES, :]
    c_act = _silu(cpre)
    c_s[...] = c_act
    for hd in range(M_HEADS):
        sl = slice(hd * M_DH, (hd + 1) * M_DH)
        ch = c_act[:, sl].astype(bf)
        q = jnp.dot(ch, wq_ref[hd], preferred_element_type=f32)
        kt = (jnp.dot(ch, wk_ref[hd], preferred_element_type=f32) * (M_DH ** -0.5)).T
        mq_s[:, sl] = q.astype(bf)
        mv_s[:, sl] = jnp.dot(zml_s[:, sl].astype(bf), wv_ref[hd], preferred_element_type=f32).astype(bf)
        for c in range(nc):
            ls = slice(c * CHUNK, (c + 1) * CHUNK)
            mkt_s[c * M_HEADS + hd] = kt[:, ls].astype(bf)
            mkw_s[c * M_HEADS + hd] = (kt[:, ls] * w_st[c][hd:hd + 1, :]).astype(bf)
            mqw_s[ls, sl] = (q[ls, :] * wi_cols[c][:, SUBLANES + hd:SUBLANES + hd + 1]).astype(bf)

    cos = cos_ref[...]
    sin = sin_ref[...]
    for c0, scale in ((0, 1.0), (R_QK, R_DK ** -0.5)):
        x1 = zret_s[:, c0:c0 + LANES]
        x2 = zret_s[:, c0 + LANES:c0 + 2 * LANES]
        zret_s[:, c0:c0 + LANES] = (x1 * cos - x2 * sin) * scale
        zret_s[:, c0 + LANES:c0 + 2 * LANES] = (x1 * sin + x2 * cos) * scale

    def log_gamma(head):
        lg = jnp.full(head.shape, math.log(1.0 - 2.0 ** -5.0), f32)
        for hd in range(1, R_HEADS):
            lg = jnp.where(head == hd, math.log(1.0 - 2.0 ** (-5.0 - hd)), lg)
        return lg

    def ret_head_of(idx):
        return (idx % LANES) // (R_DK // 2)

    n_last = float(min(CHUNK, nreal))
    lane_q = lax.broadcasted_iota(jnp.int32, (CHUNK, R_QK), 1)
    cnt_q = jnp.minimum(lax.broadcasted_iota(jnp.int32, (CHUNK, R_QK), 0) + 1, nreal).astype(f32)
    eq = jnp.exp(cnt_q * log_gamma(ret_head_of(lane_q)))
    krow = lax.broadcasted_iota(jnp.int32, (R_QK, CHUNK), 0)
    cnt_k = jnp.minimum(lax.broadcasted_iota(jnp.int32, (R_QK, CHUNK), 1) + 1, nreal).astype(f32)
    lg_k = log_gamma(ret_head_of(krow))
    ek_t = jnp.exp(-cnt_k * lg_k)
    est_t = jnp.exp((n_last - cnt_k) * lg_k)
    ret_rows = [ret_head_of(krow) == hd for hd in range(R_HEADS)]
    gla_rows = [(krow // G_DK) == hd for hd in range(G_HEADS)]
    srow = lax.broadcasted_iota(jnp.int32, (R_QK, R_V), 0)
    scol = lax.broadcasted_iota(jnp.int32, (R_QK, R_V), 1) // R_DV
    ret_blk = ret_head_of(srow) == scol
    gla_blk = (srow // G_DK) == scol
    dec_ret = jnp.exp(n_last * log_gamma(ret_head_of(srow)))

    for c in range(nc):
        rs = slice(c * CHUNK, (c + 1) * CHUNK)
        rq_s[rs, :] = (zret_s[rs, 0:R_QK] * eq).astype(bf)
        kt = zret_s[rs, R_QK:2 * R_QK].T
        kin = kt * ek_t
        for hd in range(R_HEADS):
            rkm_s[c * R_HEADS + hd] = jnp.where(ret_rows[hd], kin, 0.0).astype(bf)
        rkst_s[c] = (kt * est_t).astype(bf)
        v = zret_s[rs, 2 * R_QK:2 * R_QK + R_V].astype(bf)
        rv_s[rs, :] = v
        for p in range(2):
            rvbd_s[c * 2 + p] = _block_diag(v[:, 2 * p * R_DV:(2 * p + 1) * R_DV],
                                            v[:, (2 * p + 1) * R_DV:(2 * p + 2) * R_DV])

    la = _log_sigmoid(_dot(zs, wa2_ref[...]) + ba_ref[...]) * (1.0 / G_NORMALIZER)
    if masked:
        la = jnp.where(lax.broadcasted_iota(jnp.int32, la.shape, 0) < nreal, la, 0.0)
    for c in range(nc):
        rs = slice(c * CHUNK, (c + 1) * CHUNK)
        b = _cumsum_rows(la[rs, :], tri)
        b_t = b.T
        bl_t = b_t[:, CHUNK - 1:CHUNK]
        gq_s[rs, :] = (zgla_s[rs, 0:G_QK] * (G_DK ** -0.5) * jnp.exp(b)).astype(bf)
        kt = zgla_s[rs, G_QK:2 * G_QK].T
        kin = kt * jnp.exp(-b_t)
        for hd in range(G_HEADS):
            gkm_s[c * G_HEADS + hd] = jnp.where(gla_rows[hd], kin, 0.0).astype(bf)
        gkst_s[c] = (kt * jnp.exp(bl_t - b_t)).astype(bf)
        gdec_s[c] = jnp.broadcast_to(jnp.exp(bl_t), (G_QK, LANES))
        v = zgla_s[rs, 2 * G_QK:2 * G_QK + G_V].astype(bf)
        gv_s[rs, :] = v
        for p in range(2):
            gvbd_s[c * 2 + p] = _block_diag(v[:, 2 * p * G_DV:(2 * p + 1) * G_DV],
                                            v[:, (2 * p + 1) * G_DV:(2 * p + 2) * G_DV])

    ones_col = (lax.broadcasted_iota(jnp.int32, (CHUNK, LANES), 1) == 0).astype(bf)

    def chunk_body(c, carry):
        rows = pl.ds(pl.multiple_of(c * CHUNK, CHUNK), CHUNK)

        outs = _linear_attn_chunk(c, rows, rq_s, rkm_s, rkst_s, rv_s, rvbd_s, dec_ret, sret_o,
                                  ret_blk, causal2)
        for hd in range(R_HEADS):
            o_h = outs[hd // 2][:, (hd % 2) * R_DV:(hd % 2 + 1) * R_DV]
            gate = _silu(zret_s[rows, 2 * R_QK + R_V + hd * R_DV:2 * R_QK + R_V + (hd + 1) * R_DV])
            o_ref[0, rows, hd * R_DV:(hd + 1) * R_DV] = (_head_norm(o_h) * gate).astype(o_ref.dtype)

        dec_g = jnp.concatenate([gdec_s[c]] * G_HEADS, axis=1)
        outs = _linear_attn_chunk(c, rows, gq_s, gkm_s, gkst_s, gv_s, gvbd_s, dec_g, sgla_o,
                                  gla_blk, causal2)
        for hd in range(G_HEADS):
            o_h = outs[hd // 2][:, (hd % 2) * G_DV:(hd % 2 + 1) * G_DV]
            gate = _silu(zgla_s[rows, 2 * G_QK + G_V + hd * G_DV:2 * G_QK + G_V + (hd + 1) * G_DV])
            o_ref[0, rows, R_V + M_W + hd * G_DV:R_V + M_W + (hd + 1) * G_DV] = (
                _head_norm(o_h) * gate).astype(o_ref.dtype)

        cols = cols_s[c]
        s_old = sold_s[c]
        s_new = snew_s[c]
        for hd in range(M_HEADS):
            sl = slice(hd * M_DH, (hd + 1) * M_DH)
            sc = jnp.dot(mq_s[rows, sl], mkt_s[c * M_HEADS + hd], preferred_element_type=f32)
            sc = sc * dmat_s[c * M_HEADS + hd]
            cext = cext_o[0, hd]
            inter = jnp.dot(mqw_s[rows, sl], cext.astype(bf), preferred_element_type=f32)
            vh = mv_s[rows, sl]
            num = jnp.dot(sc.astype(bf), vh, preferred_element_type=f32) + inter[:, 0:M_DH]
            den = jnp.sum(sc, axis=1, keepdims=True) + inter[:, M_DH:M_DH + 1]
            hh = num / jnp.maximum(jnp.abs(den), cols[:, 2 * SUBLANES + hd:2 * SUBLANES + hd + 1])
            vext = jnp.concatenate([vh, ones_col], axis=1)
            dcext = jnp.dot(mkw_s[c * M_HEADS + hd], vext, preferred_element_type=f32)
            so = jnp.concatenate([s_old[hd:hd + 1, :]] * 2, axis=1)
            sn = jnp.concatenate([s_new[hd:hd + 1, :]] * 2, axis=1)
            cext_o[0, hd] = so * cext + sn * dcext
            o_m = _sigmoid(zml_s[rows, M_W + hd * M_DH:M_W + (hd + 1) * M_DH]) * (
                _head_norm(hh) + skip_ref[:, sl] * c_s[rows, sl])
            o_ref[0, rows, R_V + hd * M_DH:R_V + (hd + 1) * M_DH] = o_m.astype(o_ref.dtype)
        return carry

    lax.fori_loop(0, nc, chunk_body, 0)


def _mixer_call(x, cos, sin, init, w, *, nreal, tb):
    bsz, t, _ = x.shape
    nt = t // tb
    nc = tb // CHUNK
    sret0, cext0, m0, conv0, sgla0 = init

    def state_spec(arr):
        nd = arr.ndim
        if arr.shape[0] == 1:
            return pl.BlockSpec((1,) + arr.shape[1:], lambda b, i: (0,) * nd)
        return pl.BlockSpec((1,) + arr.shape[1:], lambda b, i: (b,) + (0,) * (nd - 1))

    def full_spec(arr):
        nd = arr.ndim
        return pl.BlockSpec(arr.shape, lambda b, i: (0,) * nd, pipeline_mode=pl.Buffered(1))

    def out_state(shape):
        nd = len(shape) + 1
        return (jax.ShapeDtypeStruct((bsz,) + shape, jnp.float32),
                pl.BlockSpec((1,) + shape, lambda b, i: (b,) + (0,) * (nd - 1)))

    weights = (w["g1"], w["wmix"], w["wa2"], w["ba"], w["gate_b"], w["conv_w"], w["conv_b"],
               w["wq"], w["wk"], w["wv"], w["skip"])
    outs = [(jax.ShapeDtypeStruct((bsz, t, R_V + M_W + G_V), MXU_DTYPE),
             pl.BlockSpec((1, tb, R_V + M_W + G_V), lambda b, i: (b, i, 0))),
            out_state((R_QK, R_V)), out_state((M_HEADS, M_DH, 2 * M_DH)), out_state((SUBLANES, LANES)),
            out_state((SUBLANES, M_W)), out_state((G_QK, G_V))]
    f32, bf = jnp.float32, MXU_DTYPE
    lin_attn = [pltpu.VMEM((tb, R_QK), bf), pltpu.VMEM((nc * 4, R_QK, CHUNK), bf),
                pltpu.VMEM((nc, R_QK, CHUNK), bf), pltpu.VMEM((tb, R_V), bf),
                pltpu.VMEM((nc * 2, 2 * R_DV, 2 * R_DV), bf)]
    scratch = ([pltpu.VMEM((tb, 2 * R_QK + 2 * R_V), f32), pltpu.VMEM((tb, 2 * M_W), f32),
                pltpu.VMEM((tb, 2 * G_QK + 2 * G_V), f32), pltpu.VMEM((tb + SUBLANES, M_W), f32),
                pltpu.VMEM((tb, M_W), f32)]
               + lin_attn + lin_attn + [pltpu.VMEM((nc, G_QK, LANES), f32)]
               + [pltpu.VMEM((tb, M_W), bf), pltpu.VMEM((tb, M_W), bf),
                  pltpu.VMEM((nc * M_HEADS, M_DH, CHUNK), bf), pltpu.VMEM((nc * M_HEADS, M_DH, CHUNK), bf),
                  pltpu.VMEM((tb, M_W), bf), pltpu.VMEM((nc * M_HEADS, CHUNK, CHUNK), f32),
                  pltpu.VMEM((nc, CHUNK, LANES), f32), pltpu.VMEM((nc, SUBLANES, LANES), f32),
                  pltpu.VMEM((nc, SUBLANES, LANES), f32)])
    return pl.pallas_call(
        functools.partial(_mixer_kernel, nreal, tb),
        grid=(bsz, nt),
        in_specs=[pl.BlockSpec((1, tb, D_MODEL), lambda b, i: (b, i, 0)),
                  pl.BlockSpec((tb, LANES), lambda b, i: (i, 0)),
                  pl.BlockSpec((tb, LANES), lambda b, i: (i, 0)),
                  state_spec(sret0), state_spec(cext0), state_spec(m0), state_spec(conv0), state_spec(sgla0)]
                 + [full_spec(a) for a in weights],
        out_specs=[o[1] for o in outs],
        out_shape=[o[0] for o in outs],
        scratch_shapes=scratch,
        compiler_params=pltpu.CompilerParams(
            dimension_semantics=("arbitrary", "arbitrary"), vmem_limit_bytes=VMEM_LIMIT_BYTES),
        name="mixer",
    )(x, cos, sin, sret0, cext0, m0, conv0, sgla0, *weights)


def _channel_kernel(final, x_ref, o_ref, g1_ref, wg_ref, wbr_ref, wout_ref, g2_ref, wfi_ref, wfo_ref,
                    gf_ref, y_ref):
    f32 = jnp.float32
    x = x_ref[...]
    h = _rmsnorm(x, g1_ref[...]).astype(MXU_DTYPE)
    mix = None
    for br, (c0, width) in enumerate(((0, R_V), (R_V, M_W), (R_V + M_W, G_V))):
        gate = _sigmoid(jnp.dot(h, wg_ref[:, br * D_MODEL:(br + 1) * D_MODEL], preferred_element_type=f32))
        p = jnp.dot(o_ref[:, c0:c0 + width], wbr_ref[c0:c0 + width, :], preferred_element_type=f32)
        mix = gate * p if mix is None else mix + gate * p
    x = x + _dot(mix, wout_ref[...])
    h2 = _rmsnorm(x, g2_ref[...]).astype(MXU_DTYPE)
    acc = x
    for j in range(D_FF // _FF_TILE):
        cs = slice(j * _FF_TILE, (j + 1) * _FF_TILE)
        ug = jnp.dot(h2, wfi_ref[:, cs], preferred_element_type=f32)
        uv = jnp.dot(h2, wfi_ref[:, D_FF + j * _FF_TILE:D_FF + (j + 1) * _FF_TILE], preferred_element_type=f32)
        acc = acc + _dot(_silu(ug) * uv, wfo_ref[cs, :])
    y_ref[...] = _rmsnorm(acc, gf_ref[...]) if final else acc


def _channel_call(x2d, o2d, w, norm_f, *, final, tm):
    n = x2d.shape[0]
    weights = (w["g1"], w["wg"], w["wbr"], w["wout"], w["g2"], w["wfi"], w["wfo"], norm_f)

    def full_spec(arr):
        nd = arr.ndim
        return pl.BlockSpec(arr.shape, lambda i: (0,) * nd, pipeline_mode=pl.Buffered(1))

    return pl.pallas_call(
        functools.partial(_channel_kernel, final),
        grid=(n // tm,),
        in_specs=[pl.BlockSpec((tm, D_MODEL), lambda i: (i, 0)),
                  pl.BlockSpec((tm, o2d.shape[1]), lambda i: (i, 0))] + [full_spec(a) for a in weights],
        out_specs=pl.BlockSpec((tm, D_MODEL), lambda i: (i, 0)),
        out_shape=jax.ShapeDtypeStruct((n, D_MODEL), jnp.float32),
        compiler_params=pltpu.CompilerParams(
            dimension_semantics=("arbitrary",), vmem_limit_bytes=VMEM_LIMIT_BYTES),
        name="channel",
    )(x2d, o2d, *weights)


def _ret_perm():
    half, hd, i = np.meshgrid(np.arange(2), np.arange(R_HEADS), np.arange(R_DK // 2), indexing="ij")
    return (hd * R_DK + half * (R_DK // 2) + i).reshape(-1)


def _prep_layer(l, norm1, w_in, b_i, b_f, conv_w, conv_b, w_mq, w_mk, w_mv, m_skip, w_a2, b_a,
                w_br_ret, w_br_mlstm, w_br_gla, w_out, norm2, w_ffn_in, w_ffn_out):
    f32 = jnp.float32
    offs = np.cumsum((0,) + IN_WIDTHS)
    cols = [w_in[l][:, offs[i]:offs[i + 1]] for i in range(len(IN_WIDTHS))]
    rq, rk, rv, rg, mx, mz, mi, mf, gq, gk, gv, gr, ga, z_ret, z_ml, z_gla = cols
    perm = _ret_perm()
    small = jnp.zeros((D_MODEL, LANES), f32)
    small = small.at[:, 0:M_HEADS].set(mi).at[:, SUBLANES:SUBLANES + M_HEADS].set(mf)
    small = small.at[:, 2 * SUBLANES:2 * SUBLANES + G_RANK].set(ga)
    wmix = jnp.concatenate([rq[:, perm], rk[:, perm], rv, rg, mx, mz, small, gq, gk, gv, gr], axis=1)
    wa2 = jnp.zeros((LANES, G_QK), f32).at[2 * SUBLANES:2 * SUBLANES + G_RANK].set(w_a2[l])
    gate_b = jnp.zeros((2 * SUBLANES, LANES), f32)
    gate_b = gate_b.at[0:M_HEADS].set(jnp.broadcast_to(b_i[l][:, None], (M_HEADS, LANES)))
    gate_b = gate_b.at[SUBLANES:SUBLANES + M_HEADS].set(jnp.broadcast_to(b_f[l][:, None], (M_HEADS, LANES)))
    bf = MXU_DTYPE
    return dict(
        g1=norm1[l][None], wmix=wmix.astype(bf), wa2=wa2.astype(bf), ba=b_a[l][None], gate_b=gate_b,
        conv_w=conv_w[l], conv_b=conv_b[l][None], wq=w_mq[l].astype(bf), wk=w_mk[l].astype(bf),
        wv=w_mv[l].astype(bf), skip=m_skip[l][None],
        wg=jnp.concatenate([z_ret, z_ml, z_gla], axis=1).astype(bf),
        wbr=jnp.concatenate([w_br_ret[l], w_br_mlstm[l], w_br_gla[l]], axis=0).astype(bf),
        wout=w_out[l].astype(bf), g2=norm2[l][None], wfi=w_ffn_in[l].astype(bf), wfo=w_ffn_out[l].astype(bf))


def _rotary_tables(pos):
    half = R_DK // 2
    inv = 1.0 / (ROPE_BASE ** jnp.linspace(0.0, 1.0, half, dtype=jnp.float32))
    ang = pos[:, None] * inv[None, :]
    return jnp.tile(jnp.cos(ang), (1, R_HEADS)), jnp.tile(jnp.sin(ang), (1, R_HEADS))


def _pack_states(s_ret, c_m, n_m, m_m, conv, s_gla):
    f32 = jnp.float32
    bsz = s_ret.shape[0]
    half = R_DK // 2
    sret = jnp.zeros((bsz, 2, R_HEADS, half, R_HEADS, R_DV), f32)
    sgla = jnp.zeros((bsz, G_HEADS, G_DK, G_HEADS, G_DV), f32)
    for hd in range(R_HEADS):
        sret = sret.at[:, :, hd, :, hd, :].set(s_ret[:, hd].astype(f32).reshape(bsz, 2, half, R_DV))
        sgla = sgla.at[:, hd, :, hd, :].set(s_gla[:, hd].astype(f32))
    cext = jnp.concatenate([c_m.astype(f32), n_m.astype(f32)[..., None],
                            jnp.zeros((bsz, M_HEADS, M_DH, M_DH - 1), f32)], axis=-1)
    m = jnp.zeros((bsz, SUBLANES, LANES), f32).at[:, 0:M_HEADS, :].set(
        jnp.broadcast_to(m_m.astype(f32)[..., None], (bsz, M_HEADS, LANES)))
    cv = jnp.zeros((bsz, SUBLANES, M_W), f32).at[:, SUBLANES - (CONV_W - 1):, :].set(conv.astype(f32))
    return (sret.reshape(bsz, R_QK, R_V), cext, m, cv, sgla.reshape(bsz, G_QK, G_V))


def _unpack_states(sret, cext, m, cv, sgla):
    bsz = sret.shape[0]
    half = R_DK // 2
    r6 = sret.reshape(bsz, 2, R_HEADS, half, R_HEADS, R_DV)
    g5 = sgla.reshape(bsz, G_HEADS, G_DK, G_HEADS, G_DV)
    s_ret = jnp.stack([r6[:, :, hd, :, hd, :].reshape(bsz, R_DK, R_DV) for hd in range(R_HEADS)], axis=1)
    s_gla = jnp.stack([g5[:, hd, :, hd, :] for hd in range(G_HEADS)], axis=1)
    return (s_ret, cext[..., 0:M_DH], cext[..., M_DH], m[:, 0:M_HEADS, 0],
            cv[:, SUBLANES - (CONV_W - 1):, :], s_gla)


def _trunk(x, pos0, nreal, init_layers, layers, norm_f, *, tb, tm):
    bsz, t, _ = x.shape
    cos, sin = _rotary_tables(pos0 + jnp.arange(t, dtype=jnp.float32))
    states = []
    for l in range(DEPTH):
        res = _mixer_call(x, cos, sin, init_layers[l], layers[l], nreal=nreal, tb=tb)
        states.append(tuple(res[1:]))
        x = _channel_call(x.reshape(bsz * t, D_MODEL), res[0].reshape(bsz * t, -1), layers[l], norm_f,
                          final=(l == DEPTH - 1), tm=tm).reshape(bsz, t, D_MODEL)
    return x, states


def kernel(x_prompt, x_sample, state_ret, state_mlstm_c, state_mlstm_n, state_mlstm_m, state_mlstm_conv, state_gla, meta_tokens, norm1, w_in, b_mlstm_i, b_mlstm_f, conv_w, conv_b, w_mq, w_mk, w_mv, m_skip, w_gla_a2, b_gla_a, w_br_ret, w_br_mlstm, w_br_gla, w_out, norm2, w_ffn_in, w_ffn_out, norm_f):
    f32 = jnp.float32
    layers = [_prep_layer(l, norm1, w_in, b_mlstm_i, b_mlstm_f, conv_w, conv_b, w_mq, w_mk, w_mv, m_skip,
                          w_gla_a2, b_gla_a, w_br_ret, w_br_mlstm, w_br_gla, w_out, norm2, w_ffn_in,
                          w_ffn_out) for l in range(DEPTH)]
    gf = norm_f[None]
    bsz, seq, _ = x_prompt.shape
    dbs, dseq, _ = x_sample.shape

    zero = _pack_states(jnp.zeros((1, R_HEADS, R_DK, R_DV), f32), jnp.zeros((1, M_HEADS, M_DH, M_DH), f32),
                        jnp.zeros((1, M_HEADS, M_DH), f32), jnp.zeros((1, M_HEADS), f32),
                        jnp.zeros((1, CONV_W - 1, M_W), f32), jnp.zeros((1, G_HEADS, G_DK, G_DV), f32))
    xm = jnp.zeros((1, CHUNK, D_MODEL), f32).at[0, 0:N_META].set(meta_tokens.astype(f32))
    _, meta_states = _trunk(xm, 0.0, N_META, [zero] * DEPTH, layers, gf, tb=CHUNK, tm=CHUNK)

    yp, p_states = _trunk(x_prompt, float(N_META), 512, meta_states, layers, gf, tb=512, tm=512)

    xs = jnp.zeros((dbs, CHUNK, D_MODEL), f32).at[:, 0:dseq].set(x_sample)
    s_init = [_pack_states(state_ret[l], state_mlstm_c[l], state_mlstm_n[l], state_mlstm_m[l],
                           state_mlstm_conv[l], state_gla[l]) for l in range(DEPTH)]
    ys, s_states = _trunk(xs, float(N_META + PAST_LEN), dseq, s_init, layers, gf, tb=CHUNK, tm=512)

    def stack(states):
        per_layer = [_unpack_states(*st) for st in states]
        return tuple(jnp.stack([pl_[i] for pl_ in per_layer]) for i in range(6))

    return (yp, ys[:, 0:dseq]) + stack(p_states) + stack(s_states)
```

```python
import functools
import math

import jax
import jax.numpy as jnp
import numpy as np
from jax import lax
from jax.experimental import pallas as pl
from jax.experimental.pallas import tpu as pltpu

D_MODEL = 1024
DEPTH = 2
N_META = 16
PAST_LEN = 2048
EPS = 1e-6
R_HEADS, R_DK, R_DV = 4, 64, 128
ROPE_BASE = 10000.0
M_HEADS, M_DH, CONV_W = 4, 128, 4
G_HEADS, G_DK, G_DV, G_RANK = 4, 64, 128, 16
G_NORMALIZER = 16.0
R_QK, R_V = R_HEADS * R_DK, R_HEADS * R_DV
M_W = M_HEADS * M_DH
G_QK, G_V = G_HEADS * G_DK, G_HEADS * G_DV
D_FF = ((-(-8 * D_MODEL // 3)) + 255) // 256 * 256
IN_WIDTHS = (R_QK, R_QK, R_V, R_V, M_W, M_W, M_HEADS, M_HEADS,
             G_QK, G_QK, G_V, G_V, G_RANK, D_MODEL, D_MODEL, D_MODEL)

LANES = 128
SUBLANES = 8
CHUNK = LANES
MXU_DTYPE = jnp.bfloat16
VMEM_LIMIT_BYTES = 56 * 1024 * 1024

_RET0, _ML0, _SM0, _GLA0, _MIXW = 0, 1536, 2560, 2688, 4224
_FF_TILE = 256


def _dot(a, b):
    return jnp.dot(a.astype(MXU_DTYPE), b.astype(MXU_DTYPE), preferred_element_type=jnp.float32)


def _rmsnorm(x, g):
    return x * lax.rsqrt(jnp.mean(x * x, axis=-1, keepdims=True) + EPS) * g


def _head_norm(o):
    return o * lax.rsqrt(jnp.mean(o * o, axis=-1, keepdims=True) + EPS)


def _sigmoid(x):
    return 1.0 / (1.0 + jnp.exp(-x))


def _silu(x):
    return x * _sigmoid(x)


def _log_sigmoid(x):
    return -(jnp.maximum(-x, 0.0) + jnp.log(1.0 + jnp.exp(-jnp.abs(x))))


def _cumsum_rows(x, tri):
    hi = x.astype(MXU_DTYPE)
    r1 = x - hi.astype(jnp.float32)
    mid = r1.astype(MXU_DTYPE)
    lo = (r1 - mid.astype(jnp.float32)).astype(MXU_DTYPE)
    f = functools.partial(jnp.dot, preferred_element_type=jnp.float32)
    return f(tri, hi) + f(tri, mid) + f(tri, lo)


def _block_diag(v0, v1):
    z = jnp.zeros_like(v0)
    return jnp.concatenate([jnp.concatenate([v0, z], axis=1), jnp.concatenate([z, v1], axis=1)], axis=0)


def _linear_attn_chunk(c, rows, q_s, km_s, kst_s, v_s, vbd_s, dec, s_ref, blk_mask, causal2):
    q_in = q_s[rows, :]
    s_prev = s_ref[...]
    o_inter = jnp.dot(q_in, s_prev.astype(MXU_DTYPE), preferred_element_type=jnp.float32)
    outs = []
    for p in range(2):
        kt = jnp.concatenate([km_s[c * 4 + 2 * p], km_s[c * 4 + 2 * p + 1]], axis=1)
        sc = jnp.dot(q_in, kt, preferred_element_type=jnp.float32)
        sc = jnp.where(causal2, sc, 0.0).astype(MXU_DTYPE)
        outs.append(jnp.dot(sc, vbd_s[c * 2 + p], preferred_element_type=jnp.float32)
                    + o_inter[:, p * 2 * LANES:(p + 1) * 2 * LANES])
    ds = jnp.dot(kst_s[c], v_s[rows, :], preferred_element_type=jnp.float32)
    s_ref[...] = dec * s_prev + jnp.where(blk_mask, ds, 0.0)
    return outs


def _ret_rows(hd, half):
    start = half * LANES + hd * (R_DK // 2)
    return slice(start, start + R_DK // 2)


def _mixer_kernel(nreal, tb, zero_init, *refs):
    n_init = 0 if zero_init else 6
    x_ref, cos_ref, sin_ref = refs[0:3]
    init = refs[3:3 + n_init]
    (g1_ref, wmix_ref, wa2_ref, ba_ref, gb_ref, cw_ref, cb_ref, wq_ref, wk_ref, wv_ref, skip_ref,
     o_ref, sret_o, c_o, n_o, m_o, conv_o, sgla_o,
     zret_s, zml_s, zgla_s, xc_s, c_s,
     rq_s, rkm_s, rkst_s, rv_s, rvbd_s,
     gq_s, gkm_s, gkst_s, gv_s, gvbd_s, gdec_s,
     mq_s, mqw_s, mkt_s, mkw_s, mv_s, dmat_s, cols_s, sold_s, snew_s,
     sret_s, cext_s, m_s, sgla_s, tmp_s) = refs[3 + n_init:]
    f32 = jnp.float32
    bf = MXU_DTYPE
    nc = tb // CHUNK
    nreal = min(nreal, tb)
    masked = nreal < tb
    lane0 = lax.broadcasted_iota(jnp.int32, (M_DH, LANES), 1) == 0

    @pl.when(pl.program_id(1) == 0)
    def _():
        sret_s[...] = jnp.zeros(sret_s.shape, f32)
        sgla_s[...] = jnp.zeros(sgla_s.shape, f32)
        xc_s[0:SUBLANES, :] = jnp.zeros((SUBLANES, M_W), f32)
        if zero_init:
            cext_s[...] = jnp.zeros(cext_s.shape, f32)
            m_s[...] = jnp.zeros(m_s.shape, f32)
        else:
            sret0, c0, n0, m0, conv0, sgla0 = init
            tmp_s[...] = jnp.zeros(tmp_s.shape, f32)
            tmp_s[0:M_HEADS, :] = n0[...]
            n_cols = tmp_s[...].T
            for hd in range(R_HEADS):
                cs = slice(hd * R_DV, (hd + 1) * R_DV)
                for half in range(2):
                    sret_s[_ret_rows(hd, half), cs] = sret0[hd, half * (R_DK // 2):(half + 1) * (R_DK // 2), :]
                sgla_s[hd * G_DK:(hd + 1) * G_DK, cs] = sgla0[hd]
                cext_s[hd, :, 0:M_DH] = c0[hd]
                cext_s[hd, :, M_DH:2 * M_DH] = jnp.where(lane0, n_cols[:, hd:hd + 1], 0.0)
            m_s[...] = m0[...]
            xc_s[SUBLANES - (CONV_W - 1):SUBLANES, :] = conv0[...]

    h = _rmsnorm(x_ref[0], g1_ref[...])
    if masked:
        h = jnp.where(lax.broadcasted_iota(jnp.int32, h.shape, 0) < nreal, h, 0.0)
    h = h.astype(bf)
    zs = jnp.dot(h, wmix_ref[:, _SM0:_GLA0], preferred_element_type=f32)
    zml_s[...] = jnp.dot(h, wmix_ref[:, _ML0:_SM0], preferred_element_type=f32)
    zret_s[...] = jnp.dot(h, wmix_ref[:, _RET0:_ML0], preferred_element_type=f32)
    zgla_s[...] = jnp.dot(h, wmix_ref[:, _GLA0:_MIXW], preferred_element_type=f32)

    row = lax.broadcasted_iota(jnp.int32, (CHUNK, CHUNK), 0)
    col = lax.broadcasted_iota(jnp.int32, (CHUNK, CHUNK), 1)
    causal = row >= col
    causal2 = jnp.concatenate([causal, causal], axis=1)
    tri = causal.astype(bf)

    zst = zs.T
    ig_all = zst[0:SUBLANES, :] + gb_ref[0:SUBLANES, 0:1]
    lf_all = _log_sigmoid(zst[SUBLANES:2 * SUBLANES, :] + gb_ref[SUBLANES:2 * SUBLANES, 0:1])
    tok = lax.broadcasted_iota(jnp.int32, ig_all.shape, 1)
    if masked:
        ig_all = jnp.where(tok < nreal, ig_all, -jnp.inf)
        lf_all = jnp.where(tok < nreal, lf_all, 0.0)
    in_chunk = tok % CHUNK

    def seg_scan(x, op, fill):
        s = 1
        while s < CHUNK:
            x = op(x, jnp.where(in_chunk >= s, pltpu.roll(x, s, axis=1), fill))
            s *= 2
        return x

    b_all = seg_scan(lf_all, jnp.add, 0.0)
    a_all = ig_all - b_all
    g_all = seg_scan(a_all, jnp.maximum, -jnp.inf)
    m_p = m_s[:, 0:1]
    w_st, wi_cols = [], []
    for c in range(nc):
        ls = slice(c * CHUNK, (c + 1) * CHUNK)
        b_c, a_c, g_c = b_all[:, ls], a_all[:, ls], g_all[:, ls]
        blast = jnp.sum(lf_all[:, ls], axis=1, keepdims=True)
        mloc = blast + jnp.max(a_c, axis=1, keepdims=True)
        m_new = jnp.maximum(blast + m_p, mloc)
        sold_s[c] = jnp.broadcast_to(jnp.exp(blast + m_p - m_new), (SUBLANES, LANES))
        snew_s[c] = jnp.broadcast_to(jnp.exp(mloc - m_new), (SUBLANES, LANES))
        w_st.append(jnp.exp(a_c + (blast - mloc)))
        m_t = b_c + jnp.maximum(m_p, g_c)
        stats = jnp.concatenate([b_c - m_t, jnp.exp(b_c + m_p - m_t), jnp.exp(-m_t),
                                 jnp.zeros((LANES - 3 * SUBLANES, LANES), f32)], axis=0)
        cols = stats.T
        cols_s[c] = cols
        wi_cols.append(cols)
        for hd in range(M_HEADS):
            dmat_s[c * M_HEADS + hd] = jnp.exp(
                jnp.where(causal, cols[:, hd:hd + 1] + a_c[hd:hd + 1, :], -jnp.inf))
        m_p = m_new
    m_s[...] = jnp.broadcast_to(m_p, (SUBLANES, LANES))

    xc_s[SUBLANES:SUBLANES + tb, :] = zml_s[:, 0:M_W]
    cpre = cb_ref[...] + sum(
        xc_s[SUBLANES - (CONV_W - 1) + j:SUBLANES - (CONV_W - 1) + j + tb, :] * cw_ref[j:j + 1, :]
        for j in range(CONV_W))
    c_act = _silu(cpre)
    c_s[...] = c_act
    for hd in range(M_HEADS):
        sl = slice(hd * M_DH, (hd + 1) * M_DH)
        ch = c_act[:, sl].astype(bf)
        q = jnp.dot(ch, wq_ref[hd], preferred_element_type=f32)
        kt = (jnp.dot(ch, wk_ref[hd], preferred_element_type=f32) * (M_DH ** -0.5)).T
        mq_s[:, sl] = q.astype(bf)
        mv_s[:, sl] = jnp.dot(zml_s[:, sl].astype(bf), wv_ref[hd], preferred_element_type=f32).astype(bf)
        for c in range(nc):
            ls = slice(c * CHUNK, (c + 1) * CHUNK)
            mkt_s[c * M_HEADS + hd] = kt[:, ls].astype(bf)
            mkw_s[c * M_HEADS + hd] = (kt[:, ls] * w_st[c][hd:hd + 1, :]).astype(bf)
            mqw_s[ls, sl] = (q[ls, :] * wi_cols[c][:, SUBLANES + hd:SUBLANES + hd + 1]).astype(bf)

    cos = cos_ref[...]
    sin = sin_ref[...]
    for c0, scale in ((0, 1.0), (R_QK, R_DK ** -0.5)):
        x1 = zret_s[:, c0:c0 + LANES]
        x2 = zret_s[:, c0 + LANES:c0 + 2 * LANES]
        zret_s[:, c0:c0 + LANES] = (x1 * cos - x2 * sin) * scale
        zret_s[:, c0 + LANES:c0 + 2 * LANES] = (x1 * sin + x2 * cos) * scale

    def log_gamma(head):
        lg = jnp.full(head.shape, math.log(1.0 - 2.0 ** -5.0), f32)
        for hd in range(1, R_HEADS):
            lg = jnp.where(head == hd, math.log(1.0 - 2.0 ** (-5.0 - hd)), lg)
        return lg

    def ret_head_of(idx):
        return (idx % LANES) // (R_DK // 2)

    n_last = float(min(CHUNK, nreal))
    lane_q = lax.broadcasted_iota(jnp.int32, (CHUNK, R_QK), 1)
    cnt_q = jnp.minimum(lax.broadcasted_iota(jnp.int32, (CHUNK, R_QK), 0) + 1, nreal).astype(f32)
    eq = jnp.exp(cnt_q * log_gamma(ret_head_of(lane_q)))
    krow = lax.broadcasted_iota(jnp.int32, (R_QK, CHUNK), 0)
    cnt_k = jnp.minimum(lax.broadcasted_iota(jnp.int32, (R_QK, CHUNK), 1) + 1, nreal).astype(f32)
    lg_k = log_gamma(ret_head_of(krow))
    ek_t = jnp.exp(-cnt_k * lg_k)
    est_t = jnp.exp((n_last - cnt_k) * lg_k)
    ret_rows = [ret_head_of(krow) == hd for hd in range(R_HEADS)]
    gla_rows = [(krow // G_DK) == hd for hd in range(G_HEADS)]
    srow = lax.broadcasted_iota(jnp.int32, (R_QK, R_V), 0)
    scol = lax.broadcasted_iota(jnp.int32, (R_QK, R_V), 1) // R_DV
    ret_blk = ret_head_of(srow) == scol
    gla_blk = (srow // G_DK) == scol
    dec_ret = jnp.exp(n_last * log_gamma(ret_head_of(srow)))

    for c in range(nc):
        rs = slice(c * CHUNK, (c + 1) * CHUNK)
        rq_s[rs, :] = (zret_s[rs, 0:R_QK] * eq).astype(bf)
        kt = zret_s[rs, R_QK:2 * R_QK].T
        kin = kt * ek_t
        for hd in range(R_HEADS):
            rkm_s[c * R_HEADS + hd] = jnp.where(ret_rows[hd], kin, 0.0).astype(bf)
        rkst_s[c] = (kt * est_t).astype(bf)
        v = zret_s[rs, 2 * R_QK:2 * R_QK + R_V].astype(bf)
        rv_s[rs, :] = v
        for p in range(2):
            rvbd_s[c * 2 + p] = _block_diag(v[:, 2 * p * R_DV:(2 * p + 1) * R_DV],
                                            v[:, (2 * p + 1) * R_DV:(2 * p + 2) * R_DV])

    la = _log_sigmoid(_dot(zs, wa2_ref[...]) + ba_ref[...]) * (1.0 / G_NORMALIZER)
    if masked:
        la = jnp.where(lax.broadcasted_iota(jnp.int32, la.shape, 0) < nreal, la, 0.0)
    for c in range(nc):
        rs = slice(c * CHUNK, (c + 1) * CHUNK)
        b = _cumsum_rows(la[rs, :], tri)
        b_t = b.T
        bl_t = b_t[:, CHUNK - 1:CHUNK]
        gq_s[rs, :] = (zgla_s[rs, 0:G_QK] * (G_DK ** -0.5) * jnp.exp(b)).astype(bf)
        kt = zgla_s[rs, G_QK:2 * G_QK].T
        kin = kt * jnp.exp(-b_t)
        for hd in range(G_HEADS):
            gkm_s[c * G_HEADS + hd] = jnp.where(gla_rows[hd], kin, 0.0).astype(bf)
        gkst_s[c] = (kt * jnp.exp(bl_t - b_t)).astype(bf)
        gdec_s[c] = jnp.broadcast_to(jnp.exp(bl_t), (G_QK, LANES))
        v = zgla_s[rs, 2 * G_QK:2 * G_QK + G_V].astype(bf)
        gv_s[rs, :] = v
        for p in range(2):
            gvbd_s[c * 2 + p] = _block_diag(v[:, 2 * p * G_DV:(2 * p + 1) * G_DV],
                                            v[:, (2 * p + 1) * G_DV:(2 * p + 2) * G_DV])

    ones_col = (lax.broadcasted_iota(jnp.int32, (CHUNK, LANES), 1) == 0).astype(bf)

    def chunk_body(c, carry):
        rows = pl.ds(pl.multiple_of(c * CHUNK, CHUNK), CHUNK)

        outs = _linear_attn_chunk(c, rows, rq_s, rkm_s, rkst_s, rv_s, rvbd_s, dec_ret, sret_s,
                                  ret_blk, causal2)
        for hd in range(R_HEADS):
            o_h = outs[hd // 2][:, (hd % 2) * R_DV:(hd % 2 + 1) * R_DV]
            gate = _silu(zret_s[rows, 2 * R_QK + R_V + hd * R_DV:2 * R_QK + R_V + (hd + 1) * R_DV])
            o_ref[0, rows, hd * R_DV:(hd + 1) * R_DV] = (_head_norm(o_h) * gate).astype(o_ref.dtype)

        dec_g = jnp.concatenate([gdec_s[c]] * G_HEADS, axis=1)
        outs = _linear_attn_chunk(c, rows, gq_s, gkm_s, gkst_s, gv_s, gvbd_s, dec_g, sgla_s,
                                  gla_blk, causal2)
        for hd in range(G_HEADS):
            o_h = outs[hd // 2][:, (hd % 2) * G_DV:(hd % 2 + 1) * G_DV]
            gate = _silu(zgla_s[rows, 2 * G_QK + G_V + hd * G_DV:2 * G_QK + G_V + (hd + 1) * G_DV])
            o_ref[0, rows, R_V + M_W + hd * G_DV:R_V + M_W + (hd + 1) * G_DV] = (
                _head_norm(o_h) * gate).astype(o_ref.dtype)

        cols = cols_s[c]
        s_old = sold_s[c]
        s_new = snew_s[c]
        for hd in range(M_HEADS):
            sl = slice(hd * M_DH, (hd + 1) * M_DH)
            sc = jnp.dot(mq_s[rows, sl], mkt_s[c * M_HEADS + hd], preferred_element_type=f32)
            sc = sc * dmat_s[c * M_HEADS + hd]
            cext = cext_s[hd]
            inter = jnp.dot(mqw_s[rows, sl], cext.astype(bf), preferred_element_type=f32)
            vh = mv_s[rows, sl]
            num = jnp.dot(sc.astype(bf), vh, preferred_element_type=f32) + inter[:, 0:M_DH]
            den = jnp.sum(sc, axis=1, keepdims=True) + inter[:, M_DH:M_DH + 1]
            hh = num / jnp.maximum(jnp.abs(den), cols[:, 2 * SUBLANES + hd:2 * SUBLANES + hd + 1])
            vext = jnp.concatenate([vh, ones_col], axis=1)
            dcext = jnp.dot(mkw_s[c * M_HEADS + hd], vext, preferred_element_type=f32)
            so = jnp.concatenate([s_old[hd:hd + 1, :]] * 2, axis=1)
            sn = jnp.concatenate([s_new[hd:hd + 1, :]] * 2, axis=1)
            cext_s[hd] = so * cext + sn * dcext
            o_m = _sigmoid(zml_s[rows, M_W + hd * M_DH:M_W + (hd + 1) * M_DH]) * (
                _head_norm(hh) + skip_ref[:, sl] * c_s[rows, sl])
            o_ref[0, rows, R_V + hd * M_DH:R_V + (hd + 1) * M_DH] = o_m.astype(o_ref.dtype)
        return carry

    lax.fori_loop(0, nc, chunk_body, 0, unroll=True)

    xc_s[0:SUBLANES, :] = xc_s[tb:tb + SUBLANES, :]

    @pl.when(pl.program_id(1) == pl.num_programs(1) - 1)
    def _():
        for hd in range(R_HEADS):
            cs = slice(hd * R_DV, (hd + 1) * R_DV)
            for half in range(2):
                sret_o[hd, half * (R_DK // 2):(half + 1) * (R_DK // 2), :] = sret_s[_ret_rows(hd, half), cs]
            sgla_o[hd] = sgla_s[hd * G_DK:(hd + 1) * G_DK, cs]
            c_o[hd] = cext_s[hd, :, 0:M_DH]
            n_o[hd:hd + 1, :] = cext_s[hd, :, M_DH:2 * M_DH].T[0:1, :]
        m_o[...] = m_s[...]
        conv_o[...] = xc_s[nreal + SUBLANES - (CONV_W - 1):nreal + SUBLANES, :]


def _layer_spec(arr, layer, grid_rank):
    shape = arr.shape[1:]
    zeros = (0,) * len(shape)
    if grid_rank == 1:
        return pl.BlockSpec((None,) + shape, lambda i: (layer,) + zeros, pipeline_mode=pl.Buffered(1))
    return pl.BlockSpec((None,) + shape, lambda b, i: (layer,) + zeros, pipeline_mode=pl.Buffered(1))


def _mixer_call(x, cos, sin, init, w, layer, *, nreal, tb):
    bsz, t, _ = x.shape
    nt = t // tb
    nc = tb // CHUNK

    def init_spec(arr, init_layer):
        shape = arr.shape[2:]
        zeros = (0,) * len(shape)
        if arr.shape[1] == 1:
            return pl.BlockSpec((None, None) + shape, lambda b, i: (init_layer, 0) + zeros)
        return pl.BlockSpec((None, None) + shape, lambda b, i: (init_layer, b) + zeros)

    def out_state(shape):
        zeros = (0,) * len(shape)
        return (jax.ShapeDtypeStruct((bsz,) + shape, jnp.float32),
                pl.BlockSpec((None,) + shape, lambda b, i: (b,) + zeros))

    weights = (w["g1"], w["wmix"], w["wa2"], w["ba"], w["gate_b"], w["conv_w"], w["conv_b"],
               w["wq"], w["wk"], w["wv"], w["skip"])
    init_arrays, init_specs = (), []
    if init is not None:
        init_arrays = tuple(init[0])
        init_specs = [init_spec(a, init[1]) for a in init_arrays]
    outs = [(jax.ShapeDtypeStruct((bsz, t, R_V + M_W + G_V), MXU_DTYPE),
             pl.BlockSpec((1, tb, R_V + M_W + G_V), lambda b, i: (b, i, 0))),
            out_state((R_HEADS, R_DK, R_DV)), out_state((M_HEADS, M_DH, M_DH)), out_state((M_HEADS, M_DH)),
            out_state((SUBLANES, LANES)), out_state((CONV_W - 1, M_W)), out_state((G_HEADS, G_DK, G_DV))]
    f32, bf = jnp.float32, MXU_DTYPE
    lin_attn = [pltpu.VMEM((tb, R_QK), bf), pltpu.VMEM((nc * 4, R_QK, CHUNK), bf),
                pltpu.VMEM((nc, R_QK, CHUNK), bf), pltpu.VMEM((tb, R_V), bf),
                pltpu.VMEM((nc * 2, 2 * R_DV, 2 * R_DV), bf)]
    scratch = ([pltpu.VMEM((tb, 2 * R_QK + 2 * R_V), f32), pltpu.VMEM((tb, 2 * M_W), f32),
                pltpu.VMEM((tb, 2 * G_QK + 2 * G_V), f32), pltpu.VMEM((tb + SUBLANES, M_W), f32),
                pltpu.VMEM((tb, M_W), f32)]
               + lin_attn + lin_attn + [pltpu.VMEM((nc, G_QK, LANES), f32)]
               + [pltpu.VMEM((tb, M_W), bf), pltpu.VMEM((tb, M_W), bf),
                  pltpu.VMEM((nc * M_HEADS, M_DH, CHUNK), bf), pltpu.VMEM((nc * M_HEADS, M_DH, CHUNK), bf),
                  pltpu.VMEM((tb, M_W), bf), pltpu.VMEM((nc * M_HEADS, CHUNK, CHUNK), f32),
                  pltpu.VMEM((nc, CHUNK, LANES), f32), pltpu.VMEM((nc, SUBLANES, LANES), f32),
                  pltpu.VMEM((nc, SUBLANES, LANES), f32)]
               + [pltpu.VMEM((R_QK, R_V), f32), pltpu.VMEM((M_HEADS, M_DH, 2 * M_DH), f32),
                  pltpu.VMEM((SUBLANES, LANES), f32), pltpu.VMEM((G_QK, G_V), f32),
                  pltpu.VMEM((M_DH, LANES), f32)])
    return pl.pallas_call(
        functools.partial(_mixer_kernel, nreal, tb, init is None),
        grid=(bsz, nt),
        in_specs=[pl.BlockSpec((1, tb, D_MODEL), lambda b, i: (b, i, 0)),
                  pl.BlockSpec((tb, LANES), lambda b, i: (i, 0)),
                  pl.BlockSpec((tb, LANES), lambda b, i: (i, 0))]
                 + init_specs + [_layer_spec(a, layer, 2) for a in weights],
        out_specs=[o[1] for o in outs],
        out_shape=[o[0] for o in outs],
        scratch_shapes=scratch,
        compiler_params=pltpu.CompilerParams(
            dimension_semantics=("arbitrary", "arbitrary"), vmem_limit_bytes=VMEM_LIMIT_BYTES),
        name="mixer",
    )(x, cos, sin, *init_arrays, *weights)


def _channel_kernel(final, x_ref, o_ref, g1_ref, wg_ref, wbr_ref, wout_ref, g2_ref, wfi_ref, wfo_ref,
                    gf_ref, y_ref):
    f32 = jnp.float32
    x = x_ref[...]
    h = _rmsnorm(x, g1_ref[...]).astype(MXU_DTYPE)
    mix = None
    for br, (c0, width) in enumerate(((0, R_V), (R_V, M_W), (R_V + M_W, G_V))):
        gate = _sigmoid(jnp.dot(h, wg_ref[:, br * D_MODEL:(br + 1) * D_MODEL], preferred_element_type=f32))
        p = jnp.dot(o_ref[:, c0:c0 + width], wbr_ref[c0:c0 + width, :], preferred_element_type=f32)
        mix = gate * p if mix is None else mix + gate * p
    x = x + _dot(mix, wout_ref[...])
    h2 = _rmsnorm(x, g2_ref[...]).astype(MXU_DTYPE)
    acc = x
    for j in range(D_FF // _FF_TILE):
        cs = slice(j * _FF_TILE, (j + 1) * _FF_TILE)
        ug = jnp.dot(h2, wfi_ref[:, cs], preferred_element_type=f32)
        uv = jnp.dot(h2, wfi_ref[:, D_FF + j * _FF_TILE:D_FF + (j + 1) * _FF_TILE], preferred_element_type=f32)
        acc = acc + _dot(_silu(ug) * uv, wfo_ref[cs, :])
    y_ref[...] = _rmsnorm(acc, gf_ref[...]) if final else acc


def _channel_call(x2d, o2d, w, layer, norm_f, *, final, tm):
    n = x2d.shape[0]
    weights = (w["g1"], w["wg"], w["wbr"], w["wout"], w["g2"], w["wfi"], w["wfo"])
    return pl.pallas_call(
        functools.partial(_channel_kernel, final),
        grid=(n // tm,),
        in_specs=[pl.BlockSpec((tm, D_MODEL), lambda i: (i, 0)),
                  pl.BlockSpec((tm, o2d.shape[1]), lambda i: (i, 0))]
                 + [_layer_spec(a, layer, 1) for a in weights]
                 + [pl.BlockSpec(norm_f.shape, lambda i: (0, 0), pipeline_mode=pl.Buffered(1))],
        out_specs=pl.BlockSpec((tm, D_MODEL), lambda i: (i, 0)),
        out_shape=jax.ShapeDtypeStruct((n, D_MODEL), jnp.float32),
        compiler_params=pltpu.CompilerParams(
            dimension_semantics=("arbitrary",), vmem_limit_bytes=VMEM_LIMIT_BYTES),
        name="channel",
    )(x2d, o2d, *weights, norm_f)


def _prep_weights(norm1, w_in, b_i, b_f, conv_w, conv_b, w_mq, w_mk, w_mv, m_skip, w_a2, b_a,
                  w_br_ret, w_br_mlstm, w_br_gla, w_out, norm2, w_ffn_in, w_ffn_out):
    f32, bf = jnp.float32, MXU_DTYPE
    depth = w_in.shape[0]
    offs = np.cumsum((0,) + IN_WIDTHS)
    rq, rk, rv, rg, mx, mz, mi, mf, gq, gk, gv, gr, ga, z_ret, z_ml, z_gla = (
        w_in[:, :, offs[i]:offs[i + 1]] for i in range(len(IN_WIDTHS)))

    def rotary_layout(wc):
        wc = wc.reshape(depth, D_MODEL, R_HEADS, 2, R_DK // 2)
        return jnp.swapaxes(wc, 2, 3).reshape(depth, D_MODEL, R_QK)

    def zcols(n):
        return jnp.zeros((depth, D_MODEL, n), f32)

    small = [mi, zcols(SUBLANES - M_HEADS), mf, zcols(SUBLANES - M_HEADS), ga,
             zcols(LANES - 2 * SUBLANES - G_RANK)]
    wmix = jnp.concatenate([rotary_layout(rq), rotary_layout(rk), rv, rg, mx, mz] + small
                           + [gq, gk, gv, gr], axis=2).astype(bf)
    wa2 = jnp.pad(w_a2, ((0, 0), (2 * SUBLANES, LANES - 2 * SUBLANES - G_RANK), (0, 0))).astype(bf)

    def gate_rows(b):
        return jnp.pad(jnp.broadcast_to(b.astype(f32)[:, :, None], (depth, M_HEADS, LANES)),
                       ((0, 0), (0, SUBLANES - M_HEADS), (0, 0)))

    return dict(
        g1=norm1[:, None, :], wmix=wmix, wa2=wa2, ba=b_a[:, None, :],
        gate_b=jnp.concatenate([gate_rows(b_i), gate_rows(b_f)], axis=1),
        conv_w=conv_w, conv_b=conv_b[:, None, :], wq=w_mq.astype(bf), wk=w_mk.astype(bf), wv=w_mv.astype(bf),
        skip=m_skip[:, None, :],
        wg=jnp.concatenate([z_ret, z_ml, z_gla], axis=2).astype(bf),
        wbr=jnp.concatenate([w_br_ret, w_br_mlstm, w_br_gla], axis=1).astype(bf),
        wout=w_out.astype(bf), g2=norm2[:, None, :], wfi=w_ffn_in.astype(bf), wfo=w_ffn_out.astype(bf))


def _rotary_tables(pos):
    half = R_DK // 2
    inv = 1.0 / (ROPE_BASE ** jnp.linspace(0.0, 1.0, half, dtype=jnp.float32))
    ang = pos[:, None] * inv[None, :]
    return jnp.tile(jnp.cos(ang), (1, R_HEADS)), jnp.tile(jnp.sin(ang), (1, R_HEADS))


def _m_rows(m):
    pad = [(0, 0)] * (m.ndim - 1) + [(0, SUBLANES - M_HEADS), (0, 0)]
    return jnp.pad(jnp.broadcast_to(m.astype(jnp.float32)[..., None], m.shape + (LANES,)), pad)


def _trunk(x, pos0, nreal, inits, w, norm_f, *, tb, tm):
    bsz, t, _ = x.shape
    cos, sin = _rotary_tables(pos0 + jnp.arange(t, dtype=jnp.float32))
    states = []
    for l in range(DEPTH):
        res = _mixer_call(x, cos, sin, inits[l], w, l, nreal=nreal, tb=tb)
        states.append(tuple(res[1:]))
        x = _channel_call(x.reshape(bsz * t, D_MODEL), res[0].reshape(bsz * t, -1), w, l, norm_f,
                          final=(l == DEPTH - 1), tm=tm).reshape(bsz, t, D_MODEL)
    return x, states


def kernel(x_prompt, x_sample, state_ret, state_mlstm_c, state_mlstm_n, state_mlstm_m, state_mlstm_conv, state_gla, meta_tokens, norm1, w_in, b_mlstm_i, b_mlstm_f, conv_w, conv_b, w_mq, w_mk, w_mv, m_skip, w_gla_a2, b_gla_a, w_br_ret, w_br_mlstm, w_br_gla, w_out, norm2, w_ffn_in, w_ffn_out, norm_f):
    f32 = jnp.float32
    w = _prep_weights(norm1, w_in, b_mlstm_i, b_mlstm_f, conv_w, conv_b, w_mq, w_mk, w_mv, m_skip,
                      w_gla_a2, b_gla_a, w_br_ret, w_br_mlstm, w_br_gla, w_out, norm2, w_ffn_in, w_ffn_out)
    gf = norm_f[None]
    dseq = x_sample.shape[1]

    xm = jnp.pad(meta_tokens.astype(f32), ((0, CHUNK - N_META), (0, 0)))[None]
    _, meta_states = _trunk(xm, 0.0, N_META, [None] * DEPTH, w, gf, tb=CHUNK, tm=CHUNK)
    p_init = [(tuple(a[None] for a in st), 0) for st in meta_states]
    yp, p_states = _trunk(x_prompt, float(N_META), x_prompt.shape[1], p_init, w, gf, tb=512, tm=512)

    xs = jnp.pad(x_sample, ((0, 0), (0, CHUNK - dseq), (0, 0)))
    s_arrays = (state_ret, state_mlstm_c, state_mlstm_n, _m_rows(state_mlstm_m), state_mlstm_conv, state_gla)
    ys, s_states = _trunk(xs, float(N_META + PAST_LEN), dseq, [(s_arrays, l) for l in range(DEPTH)], w, gf,
                          tb=CHUNK, tm=512)

    def stack(states):
        ret, c, n, m, conv, gla = (jnp.stack([st[i] for st in states]) for i in range(6))
        return ret, c, n, m[:, :, 0:M_HEADS, 0], conv, gla

    return (yp, ys[:, 0:dseq]) + stack(p_states) + stack(s_states)
```

```python
import functools
import math

import jax
import jax.numpy as jnp
import numpy as np
from jax import lax
from jax.experimental import pallas as pl
from jax.experimental.pallas import tpu as pltpu

D_MODEL = 1024
DEPTH = 2
N_META = 16
PAST_LEN = 2048
EPS = 1e-6
R_HEADS, R_DK, R_DV = 4, 64, 128
ROPE_BASE = 10000.0
M_HEADS, M_DH, CONV_W = 4, 128, 4
G_HEADS, G_DK, G_DV, G_RANK = 4, 64, 128, 16
G_NORMALIZER = 16.0
R_QK, R_V = R_HEADS * R_DK, R_HEADS * R_DV
M_W = M_HEADS * M_DH
G_QK, G_V = G_HEADS * G_DK, G_HEADS * G_DV
D_FF = ((-(-8 * D_MODEL // 3)) + 255) // 256 * 256
IN_WIDTHS = (R_QK, R_QK, R_V, R_V, M_W, M_W, M_HEADS, M_HEADS,
             G_QK, G_QK, G_V, G_V, G_RANK, D_MODEL, D_MODEL, D_MODEL)

LANES = 128
SUBLANES = 8
CHUNK = LANES
MXU_DTYPE = jnp.bfloat16
VMEM_LIMIT_BYTES = 56 * 1024 * 1024
MIXER_BLOCK = 512
CHANNEL_TILE = 512

_FF_TILE = 256


def _dot(a, b):
    return jnp.dot(a.astype(MXU_DTYPE), b.astype(MXU_DTYPE), preferred_element_type=jnp.float32)


def _rmsnorm(x, g):
    return x * lax.rsqrt(jnp.mean(x * x, axis=-1, keepdims=True) + EPS) * g


def _head_norm(o):
    return o * lax.rsqrt(jnp.mean(o * o, axis=-1, keepdims=True) + EPS)


def _sigmoid(x):
    return 1.0 / (1.0 + jnp.exp(-x))


def _silu(x):
    return x * _sigmoid(x)


def _log_sigmoid(x):
    return -(jnp.maximum(-x, 0.0) + jnp.log(1.0 + jnp.exp(-jnp.abs(x))))


def _cumsum_rows(x, tri):
    hi = x.astype(MXU_DTYPE)
    r1 = x - hi.astype(jnp.float32)
    mid = r1.astype(MXU_DTYPE)
    lo = (r1 - mid.astype(jnp.float32)).astype(MXU_DTYPE)
    f = functools.partial(jnp.dot, preferred_element_type=jnp.float32)
    return f(tri, hi) + f(tri, mid) + f(tri, lo)


def _block_diag(v0, v1):
    z = jnp.zeros_like(v0)
    return jnp.concatenate([jnp.concatenate([v0, z], axis=1), jnp.concatenate([z, v1], axis=1)], axis=0)


def _linear_attn_chunk(c, rows, q_s, km_s, kst_s, v_s, vbd_s, dec, s_ref, blk_mask, causal2):
    q_in = q_s[rows, :]
    s_prev = s_ref[...]
    o_inter = jnp.dot(q_in, s_prev.astype(MXU_DTYPE), preferred_element_type=jnp.float32)
    outs = []
    for p in range(2):
        kt = jnp.concatenate([km_s[c * 4 + 2 * p], km_s[c * 4 + 2 * p + 1]], axis=1)
        sc = jnp.dot(q_in, kt, preferred_element_type=jnp.float32)
        sc = jnp.where(causal2, sc, 0.0).astype(MXU_DTYPE)
        outs.append(jnp.dot(sc, vbd_s[c * 2 + p], preferred_element_type=jnp.float32)
                    + o_inter[:, p * 2 * LANES:(p + 1) * 2 * LANES])
    ds = jnp.dot(kst_s[c], v_s[rows, :], preferred_element_type=jnp.float32)
    s_ref[...] = dec * s_prev + jnp.where(blk_mask, ds, 0.0)
    return outs


def _ret_rows(hd, half):
    start = half * LANES + hd * (R_DK // 2)
    return slice(start, start + R_DK // 2)


def _mixer_kernel(nreal, tb, zero_init, *refs):
    n_init = 0 if zero_init else 6
    x_ref, cos_ref, sin_ref = refs[0:3]
    init = refs[3:3 + n_init]
    (g1_ref, wrqk_ref, wrv_ref, wrg_ref, wmx_ref, wmz_ref, wsm_ref, wgla_ref, wa2_ref, ba_ref, gb_ref, cw_ref, cb_ref, wq_ref, wk_ref, wv_ref, skip_ref,
     o_ref, sret_o, c_o, n_o, m_o, conv_o, sgla_o,
     zret_s, zml_s, zgla_s, xc_s, c_s,
     rq_s, rkm_s, rkst_s, rv_s, rvbd_s,
     gq_s, gkm_s, gkst_s, gv_s, gvbd_s, gdec_s,
     mq_s, mqw_s, mkt_s, mkw_s, mv_s, dmat_s, cols_s, sold_s, snew_s,
     sret_s, cext_s, m_s, sgla_s, tmp_s) = refs[3 + n_init:]
    f32 = jnp.float32
    bf = MXU_DTYPE
    nc = tb // CHUNK
    nreal = min(nreal, tb)
    masked = nreal < tb
    lane0 = lax.broadcasted_iota(jnp.int32, (M_DH, LANES), 1) == 0

    @pl.when(pl.program_id(1) == 0)
    def _():
        sret_s[...] = jnp.zeros(sret_s.shape, f32)
        sgla_s[...] = jnp.zeros(sgla_s.shape, f32)
        xc_s[0:SUBLANES, :] = jnp.zeros((SUBLANES, M_W), f32)
        if zero_init:
            cext_s[...] = jnp.zeros(cext_s.shape, f32)
            m_s[...] = jnp.zeros(m_s.shape, f32)
        else:
            sret0, c0, n0, m0, conv0, sgla0 = init
            tmp_s[...] = jnp.zeros(tmp_s.shape, f32)
            tmp_s[0:M_HEADS, :] = n0[...]
            n_cols = tmp_s[...].T
            for hd in range(R_HEADS):
                cs = slice(hd * R_DV, (hd + 1) * R_DV)
                for half in range(2):
                    sret_s[_ret_rows(hd, half), cs] = sret0[hd, half * (R_DK // 2):(half + 1) * (R_DK // 2), :]
                sgla_s[hd * G_DK:(hd + 1) * G_DK, cs] = sgla0[hd]
                cext_s[hd, :, 0:M_DH] = c0[hd]
                cext_s[hd, :, M_DH:2 * M_DH] = jnp.where(lane0, n_cols[:, hd:hd + 1], 0.0)
            m_s[...] = m0[...]
            xc_s[SUBLANES - (CONV_W - 1):SUBLANES, :] = conv0[...]

    h = _rmsnorm(x_ref[0], g1_ref[...])
    if masked:
        h = jnp.where(lax.broadcasted_iota(jnp.int32, h.shape, 0) < nreal, h, 0.0)
    h = h.astype(bf)
    zs = jnp.dot(h, wsm_ref[...], preferred_element_type=f32)
    zml_s[:, 0:M_W] = jnp.dot(h, wmx_ref[...], preferred_element_type=f32)
    zml_s[:, M_W:2 * M_W] = jnp.dot(h, wmz_ref[...], preferred_element_type=f32)
    zret_s[:, 0:2 * R_QK] = jnp.dot(h, wrqk_ref[...], preferred_element_type=f32)
    zret_s[:, 2 * R_QK:2 * R_QK + R_V] = jnp.dot(h, wrv_ref[...], preferred_element_type=f32)
    zret_s[:, 2 * R_QK + R_V:2 * R_QK + 2 * R_V] = jnp.dot(h, wrg_ref[...], preferred_element_type=f32)
    zgla_s[...] = jnp.dot(h, wgla_ref[...], preferred_element_type=f32)

    row = lax.broadcasted_iota(jnp.int32, (CHUNK, CHUNK), 0)
    col = lax.broadcasted_iota(jnp.int32, (CHUNK, CHUNK), 1)
    causal = row >= col
    causal2 = jnp.concatenate([causal, causal], axis=1)
    tri = causal.astype(bf)

    zst = zs.T
    ig_all = zst[0:SUBLANES, :] + gb_ref[0:SUBLANES, 0:1]
    lf_all = _log_sigmoid(zst[SUBLANES:2 * SUBLANES, :] + gb_ref[SUBLANES:2 * SUBLANES, 0:1])
    tok = lax.broadcasted_iota(jnp.int32, ig_all.shape, 1)
    if masked:
        ig_all = jnp.where(tok < nreal, ig_all, -jnp.inf)
        lf_all = jnp.where(tok < nreal, lf_all, 0.0)
    in_chunk = tok % CHUNK

    def seg_scan(x, op, fill):
        s = 1
        while s < CHUNK:
            x = op(x, jnp.where(in_chunk >= s, pltpu.roll(x, s, axis=1), fill))
            s *= 2
        return x

    b_all = seg_scan(lf_all, jnp.add, 0.0)
    a_all = ig_all - b_all
    g_all = seg_scan(a_all, jnp.maximum, -jnp.inf)
    m_p = m_s[:, 0:1]
    w_st, wi_cols = [], []
    for c in range(nc):
        ls = slice(c * CHUNK, (c + 1) * CHUNK)
        b_c, a_c, g_c = b_all[:, ls], a_all[:, ls], g_all[:, ls]
        blast = jnp.sum(lf_all[:, ls], axis=1, keepdims=True)
        mloc = blast + jnp.max(a_c, axis=1, keepdims=True)
        m_new = jnp.maximum(blast + m_p, mloc)
        sold_s[c] = jnp.broadcast_to(jnp.exp(blast + m_p - m_new), (SUBLANES, LANES))
        snew_s[c] = jnp.broadcast_to(jnp.exp(mloc - m_new), (SUBLANES, LANES))
        w_st.append(jnp.exp(a_c + (blast - mloc)))
        m_t = b_c + jnp.maximum(m_p, g_c)
        stats = jnp.concatenate([b_c - m_t, jnp.exp(b_c + m_p - m_t), jnp.exp(-m_t),
                                 jnp.zeros((LANES - 3 * SUBLANES, LANES), f32)], axis=0)
        cols = stats.T
        cols_s[c] = cols
        wi_cols.append(cols)
        for hd in range(M_HEADS):
            dmat_s[c * M_HEADS + hd] = jnp.exp(
                jnp.where(causal, cols[:, hd:hd + 1] + a_c[hd:hd + 1, :], -jnp.inf))
        m_p = m_new
    m_s[...] = jnp.broadcast_to(m_p, (SUBLANES, LANES))

    xc_s[SUBLANES:SUBLANES + tb, :] = zml_s[:, 0:M_W]
    cpre = cb_ref[...] + sum(
        xc_s[SUBLANES - (CONV_W - 1) + j:SUBLANES - (CONV_W - 1) + j + tb, :] * cw_ref[j:j + 1, :]
        for j in range(CONV_W))
    c_act = _silu(cpre)
    c_s[...] = c_act
    for hd in range(M_HEADS):
        sl = slice(hd * M_DH, (hd + 1) * M_DH)
        ch = c_act[:, sl].astype(bf)
        q = jnp.dot(ch, wq_ref[hd], preferred_element_type=f32)
        kt = (jnp.dot(ch, wk_ref[hd], preferred_element_type=f32) * (M_DH ** -0.5)).T
        mq_s[:, sl] = q.astype(bf)
        mv_s[:, sl] = jnp.dot(zml_s[:, sl].astype(bf), wv_ref[hd], preferred_element_type=f32).astype(bf)
        for c in range(nc):
            ls = slice(c * CHUNK, (c + 1) * CHUNK)
            mkt_s[c * M_HEADS + hd] = kt[:, ls].astype(bf)
            mkw_s[c * M_HEADS + hd] = (kt[:, ls] * w_st[c][hd:hd + 1, :]).astype(bf)
            mqw_s[ls, sl] = (q[ls, :] * wi_cols[c][:, SUBLANES + hd:SUBLANES + hd + 1]).astype(bf)

    cos = cos_ref[...]
    sin = sin_ref[...]
    for c0, scale in ((0, 1.0), (R_QK, R_DK ** -0.5)):
        x1 = zret_s[:, c0:c0 + LANES]
        x2 = zret_s[:, c0 + LANES:c0 + 2 * LANES]
        zret_s[:, c0:c0 + LANES] = (x1 * cos - x2 * sin) * scale
        zret_s[:, c0 + LANES:c0 + 2 * LANES] = (x1 * sin + x2 * cos) * scale

    def log_gamma(head):
        lg = jnp.full(head.shape, math.log(1.0 - 2.0 ** -5.0), f32)
        for hd in range(1, R_HEADS):
            lg = jnp.where(head == hd, math.log(1.0 - 2.0 ** (-5.0 - hd)), lg)
        return lg

    def ret_head_of(idx):
        return (idx % LANES) // (R_DK // 2)

    n_last = float(min(CHUNK, nreal))
    lane_q = lax.broadcasted_iota(jnp.int32, (CHUNK, R_QK), 1)
    cnt_q = jnp.minimum(lax.broadcasted_iota(jnp.int32, (CHUNK, R_QK), 0) + 1, nreal).astype(f32)
    eq = jnp.exp(cnt_q * log_gamma(ret_head_of(lane_q)))
    krow = lax.broadcasted_iota(jnp.int32, (R_QK, CHUNK), 0)
    cnt_k = jnp.minimum(lax.broadcasted_iota(jnp.int32, (R_QK, CHUNK), 1) + 1, nreal).astype(f32)
    lg_k = log_gamma(ret_head_of(krow))
    ek_t = jnp.exp(-cnt_k * lg_k)
    est_t = jnp.exp((n_last - cnt_k) * lg_k)
    ret_rows = [ret_head_of(krow) == hd for hd in range(R_HEADS)]
    gla_rows = [(krow // G_DK) == hd for hd in range(G_HEADS)]
    srow = lax.broadcasted_iota(jnp.int32, (R_QK, R_V), 0)
    scol = lax.broadcasted_iota(jnp.int32, (R_QK, R_V), 1) // R_DV
    ret_blk = ret_head_of(srow) == scol
    gla_blk = (srow // G_DK) == scol
    dec_ret = jnp.exp(n_last * log_gamma(ret_head_of(srow)))

    for c in range(nc):
        rs = slice(c * CHUNK, (c + 1) * CHUNK)
        rq_s[rs, :] = (zret_s[rs, 0:R_QK] * eq).astype(bf)
        kt = zret_s[rs, R_QK:2 * R_QK].T
        kin = kt * ek_t
        for hd in range(R_HEADS):
            rkm_s[c * R_HEADS + hd] = jnp.where(ret_rows[hd], kin, 0.0).astype(bf)
        rkst_s[c] = (kt * est_t).astype(bf)
        v = zret_s[rs, 2 * R_QK:2 * R_QK + R_V].astype(bf)
        rv_s[rs, :] = v
        for p in range(2):
            rvbd_s[c * 2 + p] = _block_diag(v[:, 2 * p * R_DV:(2 * p + 1) * R_DV],
                                            v[:, (2 * p + 1) * R_DV:(2 * p + 2) * R_DV])

    la = _log_sigmoid(_dot(zs, wa2_ref[...]) + ba_ref[...]) * (1.0 / G_NORMALIZER)
    if masked:
        la = jnp.where(lax.broadcasted_iota(jnp.int32, la.shape, 0) < nreal, la, 0.0)
    for c in range(nc):
        rs = slice(c * CHUNK, (c + 1) * CHUNK)
        b = _cumsum_rows(la[rs, :], tri)
        b_t = b.T
        bl_t = b_t[:, CHUNK - 1:CHUNK]
        gq_s[rs, :] = (zgla_s[rs, 0:G_QK] * (G_DK ** -0.5) * jnp.exp(b)).astype(bf)
        kt = zgla_s[rs, G_QK:2 * G_QK].T
        kin = kt * jnp.exp(-b_t)
        for hd in range(G_HEADS):
            gkm_s[c * G_HEADS + hd] = jnp.where(gla_rows[hd], kin, 0.0).astype(bf)
        gkst_s[c] = (kt * jnp.exp(bl_t - b_t)).astype(bf)
        gdec_s[c] = jnp.broadcast_to(jnp.exp(bl_t), (G_QK, LANES))
        v = zgla_s[rs, 2 * G_QK:2 * G_QK + G_V].astype(bf)
        gv_s[rs, :] = v
        for p in range(2):
            gvbd_s[c * 2 + p] = _block_diag(v[:, 2 * p * G_DV:(2 * p + 1) * G_DV],
                                            v[:, (2 * p + 1) * G_DV:(2 * p + 2) * G_DV])

    ones_col = (lax.broadcasted_iota(jnp.int32, (CHUNK, LANES), 1) == 0).astype(bf)

    def chunk_body(c, carry):
        rows = pl.ds(pl.multiple_of(c * CHUNK, CHUNK), CHUNK)

        outs = _linear_attn_chunk(c, rows, rq_s, rkm_s, rkst_s, rv_s, rvbd_s, dec_ret, sret_s,
                                  ret_blk, causal2)
        for hd in range(R_HEADS):
            o_h = outs[hd // 2][:, (hd % 2) * R_DV:(hd % 2 + 1) * R_DV]
            gate = _silu(zret_s[rows, 2 * R_QK + R_V + hd * R_DV:2 * R_QK + R_V + (hd + 1) * R_DV])
            o_ref[0, rows, hd * R_DV:(hd + 1) * R_DV] = (_head_norm(o_h) * gate).astype(o_ref.dtype)

        dec_g = jnp.concatenate([gdec_s[c]] * G_HEADS, axis=1)
        outs = _linear_attn_chunk(c, rows, gq_s, gkm_s, gkst_s, gv_s, gvbd_s, dec_g, sgla_s,
                                  gla_blk, causal2)
        for hd in range(G_HEADS):
            o_h = outs[hd // 2][:, (hd % 2) * G_DV:(hd % 2 + 1) * G_DV]
            gate = _silu(zgla_s[rows, 2 * G_QK + G_V + hd * G_DV:2 * G_QK + G_V + (hd + 1) * G_DV])
            o_ref[0, rows, R_V + M_W + hd * G_DV:R_V + M_W + (hd + 1) * G_DV] = (
                _head_norm(o_h) * gate).astype(o_ref.dtype)

        cols = cols_s[c]
        s_old = sold_s[c]
        s_new = snew_s[c]
        for hd in range(M_HEADS):
            sl = slice(hd * M_DH, (hd + 1) * M_DH)
            sc = jnp.dot(mq_s[rows, sl], mkt_s[c * M_HEADS + hd], preferred_element_type=f32)
            sc = sc * dmat_s[c * M_HEADS + hd]
            cext = cext_s[hd]
            inter = jnp.dot(mqw_s[rows, sl], cext.astype(bf), preferred_element_type=f32)
            vh = mv_s[rows, sl]
            num = jnp.dot(sc.astype(bf), vh, preferred_element_type=f32) + inter[:, 0:M_DH]
            den = jnp.sum(sc, axis=1, keepdims=True) + inter[:, M_DH:M_DH + 1]
            hh = num / jnp.maximum(jnp.abs(den), cols[:, 2 * SUBLANES + hd:2 * SUBLANES + hd + 1])
            vext = jnp.concatenate([vh, ones_col], axis=1)
            dcext = jnp.dot(mkw_s[c * M_HEADS + hd], vext, preferred_element_type=f32)
            so = jnp.concatenate([s_old[hd:hd + 1, :]] * 2, axis=1)
            sn = jnp.concatenate([s_new[hd:hd + 1, :]] * 2, axis=1)
            cext_s[hd] = so * cext + sn * dcext
            o_m = _sigmoid(zml_s[rows, M_W + hd * M_DH:M_W + (hd + 1) * M_DH]) * (
                _head_norm(hh) + skip_ref[:, sl] * c_s[rows, sl])
            o_ref[0, rows, R_V + hd * M_DH:R_V + (hd + 1) * M_DH] = o_m.astype(o_ref.dtype)
        return carry

    lax.fori_loop(0, nc, chunk_body, 0, unroll=True)

    xc_s[0:SUBLANES, :] = xc_s[tb:tb + SUBLANES, :]

    @pl.when(pl.program_id(1) == pl.num_programs(1) - 1)
    def _():
        for hd in range(R_HEADS):
            cs = slice(hd * R_DV, (hd + 1) * R_DV)
            for half in range(2):
                sret_o[hd, half * (R_DK // 2):(half + 1) * (R_DK // 2), :] = sret_s[_ret_rows(hd, half), cs]
            sgla_o[hd] = sgla_s[hd * G_DK:(hd + 1) * G_DK, cs]
            c_o[hd] = cext_s[hd, :, 0:M_DH]
            n_o[hd:hd + 1, :] = cext_s[hd, :, M_DH:2 * M_DH].T[0:1, :]
        m_o[...] = m_s[...]
        conv_o[...] = xc_s[nreal + SUBLANES - (CONV_W - 1):nreal + SUBLANES, :]


def _layer_spec(arr, layer, grid_rank):
    shape = arr.shape[1:]
    zeros = (0,) * len(shape)
    if grid_rank == 1:
        return pl.BlockSpec((None,) + shape, lambda i: (layer,) + zeros, pipeline_mode=pl.Buffered(1))
    return pl.BlockSpec((None,) + shape, lambda b, i: (layer,) + zeros, pipeline_mode=pl.Buffered(1))


def _mixer_call(x, cos, sin, init, w, layer, *, nreal, tb):
    bsz, t, _ = x.shape
    nt = t // tb
    nc = tb // CHUNK

    def init_spec(arr, init_layer):
        shape = arr.shape[2:]
        zeros = (0,) * len(shape)
        if arr.shape[1] == 1:
            return pl.BlockSpec((None, None) + shape, lambda b, i: (init_layer, 0) + zeros)
        return pl.BlockSpec((None, None) + shape, lambda b, i: (init_layer, b) + zeros)

    def out_state(shape):
        zeros = (0,) * len(shape)
        return (jax.ShapeDtypeStruct((bsz,) + shape, jnp.float32),
                pl.BlockSpec((None,) + shape, lambda b, i: (b,) + zeros))

    def win_spec(k):
        return pl.BlockSpec((None, D_MODEL, R_V), lambda b, i: (layer, 0, k), pipeline_mode=pl.Buffered(1))

    assert IN_WIDTHS[:6] == (R_V // 2, R_V // 2, R_V, R_V, R_V, R_V)
    tail = (w["wsmall"], w["wgla"], w["wa2"], w["ba"], w["gate_b"], w["conv_w"], w["conv_b"],
            w["wq"], w["wk"], w["wv"], w["skip"])
    weights = (w["g1"], w["wrqk"]) + (w["win"],) * 4 + tail
    weight_specs = ([_layer_spec(w["g1"], layer, 2), _layer_spec(w["wrqk"], layer, 2)]
                    + [win_spec(k) for k in range(1, 5)] + [_layer_spec(a, layer, 2) for a in tail])
    init_arrays, init_specs = (), []
    if init is not None:
        init_arrays = tuple(init[0])
        init_specs = [init_spec(a, init[1]) for a in init_arrays]
    outs = [(jax.ShapeDtypeStruct((bsz, t, R_V + M_W + G_V), MXU_DTYPE),
             pl.BlockSpec((1, tb, R_V + M_W + G_V), lambda b, i: (b, i, 0))),
            out_state((R_HEADS, R_DK, R_DV)), out_state((M_HEADS, M_DH, M_DH)), out_state((M_HEADS, M_DH)),
            out_state((SUBLANES, LANES)), out_state((CONV_W - 1, M_W)), out_state((G_HEADS, G_DK, G_DV))]
    f32, bf = jnp.float32, MXU_DTYPE
    lin_attn = [pltpu.VMEM((tb, R_QK), bf), pltpu.VMEM((nc * 4, R_QK, CHUNK), bf),
                pltpu.VMEM((nc, R_QK, CHUNK), bf), pltpu.VMEM((tb, R_V), bf),
                pltpu.VMEM((nc * 2, 2 * R_DV, 2 * R_DV), bf)]
    scratch = ([pltpu.VMEM((tb, 2 * R_QK + 2 * R_V), f32), pltpu.VMEM((tb, 2 * M_W), f32),
                pltpu.VMEM((tb, 2 * G_QK + 2 * G_V), f32), pltpu.VMEM((tb + SUBLANES, M_W), f32),
                pltpu.VMEM((tb, M_W), f32)]
               + lin_attn + lin_attn + [pltpu.VMEM((nc, G_QK, LANES), f32)]
               + [pltpu.VMEM((tb, M_W), bf), pltpu.VMEM((tb, M_W), bf),
                  pltpu.VMEM((nc * M_HEADS, M_DH, CHUNK), bf), pltpu.VMEM((nc * M_HEADS, M_DH, CHUNK), bf),
                  pltpu.VMEM((tb, M_W), bf), pltpu.VMEM((nc * M_HEADS, CHUNK, CHUNK), f32),
                  pltpu.VMEM((nc, CHUNK, LANES), f32), pltpu.VMEM((nc, SUBLANES, LANES), f32),
                  pltpu.VMEM((nc, SUBLANES, LANES), f32)]
               + [pltpu.VMEM((R_QK, R_V), f32), pltpu.VMEM((M_HEADS, M_DH, 2 * M_DH), f32),
                  pltpu.VMEM((SUBLANES, LANES), f32), pltpu.VMEM((G_QK, G_V), f32),
                  pltpu.VMEM((M_DH, LANES), f32)])
    return pl.pallas_call(
        functools.partial(_mixer_kernel, nreal, tb, init is None),
        grid=(bsz, nt),
        in_specs=[pl.BlockSpec((1, tb, D_MODEL), lambda b, i: (b, i, 0)),
                  pl.BlockSpec((tb, LANES), lambda b, i: (i, 0)),
                  pl.BlockSpec((tb, LANES), lambda b, i: (i, 0))]
                 + init_specs + weight_specs,
        out_specs=[o[1] for o in outs],
        out_shape=[o[0] for o in outs],
        scratch_shapes=scratch,
        compiler_params=pltpu.CompilerParams(
            dimension_semantics=("arbitrary", "arbitrary"), vmem_limit_bytes=VMEM_LIMIT_BYTES),
        name="mixer",
    )(x, cos, sin, *init_arrays, *weights)


def _channel_kernel(final, x_ref, o_ref, g1_ref, wg_ref, wbr_ref, wout_ref, g2_ref, wfi_ref, wfo_ref,
                    gf_ref, y_ref):
    f32 = jnp.float32
    x = x_ref[...]
    h = _rmsnorm(x, g1_ref[...]).astype(MXU_DTYPE)
    mix = None
    for br, (c0, width) in enumerate(((0, R_V), (R_V, M_W), (R_V + M_W, G_V))):
        gate = _sigmoid(jnp.dot(h, wg_ref[:, br * D_MODEL:(br + 1) * D_MODEL], preferred_element_type=f32))
        p = jnp.dot(o_ref[:, c0:c0 + width], wbr_ref[c0:c0 + width, :], preferred_element_type=f32)
        mix = gate * p if mix is None else mix + gate * p
    x = x + _dot(mix, wout_ref[...])
    h2 = _rmsnorm(x, g2_ref[...]).astype(MXU_DTYPE)
    acc = x
    for j in range(D_FF // _FF_TILE):
        cs = slice(j * _FF_TILE, (j + 1) * _FF_TILE)
        ug = jnp.dot(h2, wfi_ref[:, cs], preferred_element_type=f32)
        uv = jnp.dot(h2, wfi_ref[:, D_FF + j * _FF_TILE:D_FF + (j + 1) * _FF_TILE], preferred_element_type=f32)
        acc = acc + _dot(_silu(ug) * uv, wfo_ref[cs, :])
    y_ref[...] = _rmsnorm(acc, gf_ref[...]) if final else acc


def _channel_call(x2d, o2d, w, layer, norm_f, *, final, tm):
    n = x2d.shape[0]
    weights = (w["g1"], w["wg"], w["wbr"], w["wout"], w["g2"], w["wfi"], w["wfo"])
    return pl.pallas_call(
        functools.partial(_channel_kernel, final),
        grid=(n // tm,),
        in_specs=[pl.BlockSpec((tm, D_MODEL), lambda i: (i, 0)),
                  pl.BlockSpec((tm, o2d.shape[1]), lambda i: (i, 0))]
                 + [_layer_spec(a, layer, 1) for a in weights]
                 + [pl.BlockSpec(norm_f.shape, lambda i: (0, 0), pipeline_mode=pl.Buffered(1))],
        out_specs=pl.BlockSpec((tm, D_MODEL), lambda i: (i, 0)),
        out_shape=jax.ShapeDtypeStruct((n, D_MODEL), jnp.float32),
        compiler_params=pltpu.CompilerParams(
            dimension_semantics=("arbitrary",), vmem_limit_bytes=VMEM_LIMIT_BYTES),
        name="channel",
    )(x2d, o2d, *weights, norm_f)


def _prep_weights(norm1, w_in, b_i, b_f, conv_w, conv_b, w_mq, w_mk, w_mv, m_skip, w_a2, b_a,
                  w_br_ret, w_br_mlstm, w_br_gla, w_out, norm2, w_ffn_in, w_ffn_out):
    f32, bf = jnp.float32, MXU_DTYPE
    depth = w_in.shape[0]
    offs = np.cumsum((0,) + IN_WIDTHS)
    win = w_in.astype(bf)
    rq, rk, rv, rg, mx, mz, mi, mf, gq, gk, gv, gr, ga, z_ret, z_ml, z_gla = (
        win[:, :, offs[i]:offs[i + 1]] for i in range(len(IN_WIDTHS)))

    def rotary_layout(wc):
        wc = wc.reshape(depth, D_MODEL, R_HEADS, 2, R_DK // 2)
        return jnp.swapaxes(wc, 2, 3).reshape(depth, D_MODEL, R_QK)

    def zcols(n):
        return jnp.zeros((depth, D_MODEL, n), bf)

    wsmall = jnp.concatenate([mi, zcols(SUBLANES - M_HEADS), mf, zcols(SUBLANES - M_HEADS), ga,
                              zcols(LANES - 2 * SUBLANES - G_RANK)], axis=2)
    wa2 = jnp.pad(w_a2, ((0, 0), (2 * SUBLANES, LANES - 2 * SUBLANES - G_RANK), (0, 0))).astype(bf)

    def gate_rows(b):
        return jnp.pad(jnp.broadcast_to(b.astype(f32)[:, :, None], (depth, M_HEADS, LANES)),
                       ((0, 0), (0, SUBLANES - M_HEADS), (0, 0)))

    return dict(
        g1=norm1[:, None, :], win=win, wrqk=jnp.concatenate([rotary_layout(rq), rotary_layout(rk)], axis=2),
        wsmall=wsmall, wgla=jnp.concatenate([gq, gk, gv, gr], axis=2), wa2=wa2, ba=b_a[:, None, :],
        gate_b=jnp.concatenate([gate_rows(b_i), gate_rows(b_f)], axis=1),
        conv_w=conv_w, conv_b=conv_b[:, None, :], wq=w_mq.astype(bf), wk=w_mk.astype(bf), wv=w_mv.astype(bf),
        skip=m_skip[:, None, :],
        wg=jnp.concatenate([z_ret, z_ml, z_gla], axis=2),
        wbr=jnp.concatenate([w_br_ret, w_br_mlstm, w_br_gla], axis=1).astype(bf),
        wout=w_out.astype(bf), g2=norm2[:, None, :], wfi=w_ffn_in.astype(bf), wfo=w_ffn_out.astype(bf))


def _rotary_tables(pos):
    half = R_DK // 2
    inv = 1.0 / (ROPE_BASE ** jnp.linspace(0.0, 1.0, half, dtype=jnp.float32))
    ang = pos[:, None] * inv[None, :]
    return jnp.tile(jnp.cos(ang), (1, R_HEADS)), jnp.tile(jnp.sin(ang), (1, R_HEADS))


def _m_rows(m):
    pad = [(0, 0)] * (m.ndim - 1) + [(0, SUBLANES - M_HEADS), (0, 0)]
    return jnp.pad(jnp.broadcast_to(m.astype(jnp.float32)[..., None], m.shape + (LANES,)), pad)


def _trunk(x, pos0, inits, w, norm_f, *, tb, tm):
    bsz, n, _ = x.shape
    t = -(-n // CHUNK) * CHUNK
    rows = bsz * n
    cos, sin = _rotary_tables(pos0 + jnp.arange(t, dtype=jnp.float32))
    states = []
    for l in range(DEPTH):
        xp = x if t == n else jnp.pad(x, ((0, 0), (0, t - n), (0, 0)))
        res = _mixer_call(xp, cos, sin, inits[l], w, l, nreal=n, tb=min(tb, t))
        states.append(tuple(res[1:]))
        o = res[0] if t == n else res[0][:, 0:n]
        x = _channel_call(x.reshape(rows, D_MODEL), o.reshape(rows, -1), w, l, norm_f,
                          final=(l == DEPTH - 1), tm=min(tm, rows)).reshape(bsz, n, D_MODEL)
    return x, states


def kernel(x_prompt, x_sample, state_ret, state_mlstm_c, state_mlstm_n, state_mlstm_m, state_mlstm_conv, state_gla, meta_tokens, norm1, w_in, b_mlstm_i, b_mlstm_f, conv_w, conv_b, w_mq, w_mk, w_mv, m_skip, w_gla_a2, b_gla_a, w_br_ret, w_br_mlstm, w_br_gla, w_out, norm2, w_ffn_in, w_ffn_out, norm_f):
    f32 = jnp.float32
    w = _prep_weights(norm1, w_in, b_mlstm_i, b_mlstm_f, conv_w, conv_b, w_mq, w_mk, w_mv, m_skip,
                      w_gla_a2, b_gla_a, w_br_ret, w_br_mlstm, w_br_gla, w_out, norm2, w_ffn_in, w_ffn_out)
    gf = norm_f[None]
    tiles = dict(tb=MIXER_BLOCK, tm=CHANNEL_TILE)

    _, meta_states = _trunk(meta_tokens.astype(f32)[None], 0.0, [None] * DEPTH, w, gf, **tiles)
    p_init = [(tuple(a[None] for a in st), 0) for st in meta_states]
    yp, p_states = _trunk(x_prompt, float(N_META), p_init, w, gf, **tiles)

    s_arrays = (state_ret, state_mlstm_c, state_mlstm_n, _m_rows(state_mlstm_m), state_mlstm_conv, state_gla)
    ys, s_states = _trunk(x_sample, float(N_META + PAST_LEN), [(s_arrays, l) for l in range(DEPTH)], w, gf,
                          **tiles)

    def stack(states):
        ret, c, n, m, conv, gla = (jnp.stack([st[i] for st in states]) for i in range(6))
        return ret, c, n, m[:, :, 0:M_HEADS, 0], conv, gla

    return (yp, ys) + stack(p_states) + stack(s_states)
```

```python
import functools
import math

import jax
import jax.numpy as jnp
import numpy as np
from jax import lax
from jax.experimental import pallas as pl
from jax.experimental.pallas import tpu as pltpu

D_MODEL = 1024
DEPTH = 2
N_META = 16
PAST_LEN = 2048
EPS = 1e-6
R_HEADS, R_DK, R_DV = 4, 64, 128
ROPE_BASE = 10000.0
M_HEADS, M_DH, CONV_W = 4, 128, 4
G_HEADS, G_DK, G_DV, G_RANK = 4, 64, 128, 16
G_NORMALIZER = 16.0
R_QK, R_V = R_HEADS * R_DK, R_HEADS * R_DV
M_W = M_HEADS * M_DH
G_QK, G_V = G_HEADS * G_DK, G_HEADS * G_DV
D_FF = ((-(-8 * D_MODEL // 3)) + 255) // 256 * 256
IN_WIDTHS = (R_QK, R_QK, R_V, R_V, M_W, M_W, M_HEADS, M_HEADS,
             G_QK, G_QK, G_V, G_V, G_RANK, D_MODEL, D_MODEL, D_MODEL)

LANES = 128
SUBLANES = 8
CHUNK = LANES
MXU_DTYPE = jnp.bfloat16
VMEM_LIMIT_BYTES = 56 * 1024 * 1024
MIXER_BLOCK = 512
CHANNEL_TILE = 512

_FF_TILE = 256


def _dot(a, b):
    return jnp.dot(a.astype(MXU_DTYPE), b.astype(MXU_DTYPE), preferred_element_type=jnp.float32)


def _rmsnorm(x, g):
    return x * lax.rsqrt(jnp.mean(x * x, axis=-1, keepdims=True) + EPS) * g


def _head_norm(o):
    return o * lax.rsqrt(jnp.mean(o * o, axis=-1, keepdims=True) + EPS)


def _sigmoid(x):
    return 1.0 / (1.0 + jnp.exp(-x))


def _silu(x):
    return x * _sigmoid(x)


def _log_sigmoid(x):
    return -(jnp.maximum(-x, 0.0) + jnp.log(1.0 + jnp.exp(-jnp.abs(x))))


def _cumsum_rows(x, tri):
    hi = x.astype(MXU_DTYPE)
    r1 = x - hi.astype(jnp.float32)
    mid = r1.astype(MXU_DTYPE)
    lo = (r1 - mid.astype(jnp.float32)).astype(MXU_DTYPE)
    f = functools.partial(jnp.dot, preferred_element_type=jnp.float32)
    return f(tri, hi) + f(tri, mid) + f(tri, lo)


def _block_diag(v0, v1):
    z = jnp.zeros_like(v0)
    return jnp.concatenate([jnp.concatenate([v0, z], axis=1), jnp.concatenate([z, v1], axis=1)], axis=0)


def _linear_attn_chunk(c, rows, q_s, km_s, kst_s, v_s, vbd_s, dec, s_ref, blk_mask, causal2):
    q_in = q_s[rows, :]
    s_prev = s_ref[...]
    o_inter = jnp.dot(q_in, s_prev.astype(MXU_DTYPE), preferred_element_type=jnp.float32)
    outs = []
    for p in range(2):
        kt = jnp.concatenate([km_s[c * 4 + 2 * p], km_s[c * 4 + 2 * p + 1]], axis=1)
        sc = jnp.dot(q_in, kt, preferred_element_type=jnp.float32)
        sc = jnp.where(causal2, sc, 0.0).astype(MXU_DTYPE)
        outs.append(jnp.dot(sc, vbd_s[c * 2 + p], preferred_element_type=jnp.float32)
                    + o_inter[:, p * 2 * LANES:(p + 1) * 2 * LANES])
    ds = jnp.dot(kst_s[c], v_s[rows, :], preferred_element_type=jnp.float32)
    s_ref[...] = dec * s_prev + jnp.where(blk_mask, ds, 0.0)
    return outs


def _ret_rows(hd, half):
    start = half * LANES + hd * (R_DK // 2)
    return slice(start, start + R_DK // 2)


def _mixer_kernel(nreal, tb, zero_init, *refs):
    n_init = 0 if zero_init else 6
    x_ref, cos_ref, sin_ref = refs[0:3]
    init = refs[3:3 + n_init]
    (g1_ref, wrqk_ref, wrv_ref, wrg_ref, wmx_ref, wmz_ref, wsm_ref, wgla_ref, wa2_ref, ba_ref, gb_ref, cw_ref, cb_ref, wq_ref, wk_ref, wv_ref, skip_ref,
     wg_ref, wbr_ref,
     mix_ref, sret_o, c_o, n_o, m_o, conv_o, sgla_o,
     zret_s, zml_s, zgla_s, xc_s, c_s,
     rq_s, rkm_s, rkst_s, rv_s, rvbd_s,
     gq_s, gkm_s, gkst_s, gv_s, gvbd_s, gdec_s,
     mq_s, mqw_s, mkt_s, mkw_s, mv_s, dmat_s, cols_s, sold_s, snew_s,
     sret_s, cext_s, m_s, sgla_s, tmp_s, o_s) = refs[3 + n_init:]
    f32 = jnp.float32
    bf = MXU_DTYPE
    nc = tb // CHUNK
    nreal = min(nreal, tb)
    masked = nreal < tb
    lane0 = lax.broadcasted_iota(jnp.int32, (M_DH, LANES), 1) == 0

    @pl.when(pl.program_id(1) == 0)
    def _():
        sret_s[...] = jnp.zeros(sret_s.shape, f32)
        sgla_s[...] = jnp.zeros(sgla_s.shape, f32)
        xc_s[0:SUBLANES, :] = jnp.zeros((SUBLANES, M_W), f32)
        if zero_init:
            cext_s[...] = jnp.zeros(cext_s.shape, f32)
            m_s[...] = jnp.zeros(m_s.shape, f32)
        else:
            sret0, c0, n0, m0, conv0, sgla0 = init
            tmp_s[...] = jnp.zeros(tmp_s.shape, f32)
            tmp_s[0:M_HEADS, :] = n0[...]
            n_cols = tmp_s[...].T
            for hd in range(R_HEADS):
                cs = slice(hd * R_DV, (hd + 1) * R_DV)
                for half in range(2):
                    sret_s[_ret_rows(hd, half), cs] = sret0[hd, half * (R_DK // 2):(half + 1) * (R_DK // 2), :]
                sgla_s[hd * G_DK:(hd + 1) * G_DK, cs] = sgla0[hd]
                cext_s[hd, :, 0:M_DH] = c0[hd]
                cext_s[hd, :, M_DH:2 * M_DH] = jnp.where(lane0, n_cols[:, hd:hd + 1], 0.0)
            m_s[...] = m0[...]
            xc_s[SUBLANES - (CONV_W - 1):SUBLANES, :] = conv0[...]

    h = _rmsnorm(x_ref[0], g1_ref[...])
    if masked:
        h = jnp.where(lax.broadcasted_iota(jnp.int32, h.shape, 0) < nreal, h, 0.0)
    h = h.astype(bf)
    gates = [_sigmoid(jnp.dot(h, wg_ref[:, br * D_MODEL:(br + 1) * D_MODEL], preferred_element_type=f32))
             for br in range(3)]
    zs = jnp.dot(h, wsm_ref[...], preferred_element_type=f32)
    zml_s[:, 0:M_W] = jnp.dot(h, wmx_ref[...], preferred_element_type=f32)
    zml_s[:, M_W:2 * M_W] = jnp.dot(h, wmz_ref[...], preferred_element_type=f32)
    zret_s[:, 0:2 * R_QK] = jnp.dot(h, wrqk_ref[...], preferred_element_type=f32)
    zret_s[:, 2 * R_QK:2 * R_QK + R_V] = jnp.dot(h, wrv_ref[...], preferred_element_type=f32)
    zret_s[:, 2 * R_QK + R_V:2 * R_QK + 2 * R_V] = jnp.dot(h, wrg_ref[...], preferred_element_type=f32)
    zgla_s[...] = jnp.dot(h, wgla_ref[...], preferred_element_type=f32)

    row = lax.broadcasted_iota(jnp.int32, (CHUNK, CHUNK), 0)
    col = lax.broadcasted_iota(jnp.int32, (CHUNK, CHUNK), 1)
    causal = row >= col
    causal2 = jnp.concatenate([causal, causal], axis=1)
    tri = causal.astype(bf)

    zst = zs.T
    ig_all = zst[0:SUBLANES, :] + gb_ref[0:SUBLANES, 0:1]
    lf_all = _log_sigmoid(zst[SUBLANES:2 * SUBLANES, :] + gb_ref[SUBLANES:2 * SUBLANES, 0:1])
    tok = lax.broadcasted_iota(jnp.int32, ig_all.shape, 1)
    if masked:
        ig_all = jnp.where(tok < nreal, ig_all, -jnp.inf)
        lf_all = jnp.where(tok < nreal, lf_all, 0.0)
    in_chunk = tok % CHUNK

    def seg_scan(x, op, fill):
        s = 1
        while s < CHUNK:
            x = op(x, jnp.where(in_chunk >= s, pltpu.roll(x, s, axis=1), fill))
            s *= 2
        return x

    b_all = seg_scan(lf_all, jnp.add, 0.0)
    a_all = ig_all - b_all
    g_all = seg_scan(a_all, jnp.maximum, -jnp.inf)
    m_p = m_s[:, 0:1]
    w_st, wi_cols = [], []
    for c in range(nc):
        ls = slice(c * CHUNK, (c + 1) * CHUNK)
        b_c, a_c, g_c = b_all[:, ls], a_all[:, ls], g_all[:, ls]
        blast = jnp.sum(lf_all[:, ls], axis=1, keepdims=True)
        mloc = blast + jnp.max(a_c, axis=1, keepdims=True)
        m_new = jnp.maximum(blast + m_p, mloc)
        sold_s[c] = jnp.broadcast_to(jnp.exp(blast + m_p - m_new), (SUBLANES, LANES))
        snew_s[c] = jnp.broadcast_to(jnp.exp(mloc - m_new), (SUBLANES, LANES))
        w_st.append(jnp.exp(a_c + (blast - mloc)))
        m_t = b_c + jnp.maximum(m_p, g_c)
        stats = jnp.concatenate([b_c - m_t, jnp.exp(b_c + m_p - m_t), jnp.exp(-m_t),
                                 jnp.zeros((LANES - 3 * SUBLANES, LANES), f32)], axis=0)
        cols = stats.T
        cols_s[c] = cols
        wi_cols.append(cols)
        for hd in range(M_HEADS):
            dmat_s[c * M_HEADS + hd] = jnp.exp(
                jnp.where(causal, cols[:, hd:hd + 1] + a_c[hd:hd + 1, :], -jnp.inf))
        m_p = m_new
    m_s[...] = jnp.broadcast_to(m_p, (SUBLANES, LANES))

    xc_s[SUBLANES:SUBLANES + tb, :] = zml_s[:, 0:M_W]
    cpre = cb_ref[...] + sum(
        xc_s[SUBLANES - (CONV_W - 1) + j:SUBLANES - (CONV_W - 1) + j + tb, :] * cw_ref[j:j + 1, :]
        for j in range(CONV_W))
    c_act = _silu(cpre)
    c_s[...] = c_act
    for hd in range(M_HEADS):
        sl = slice(hd * M_DH, (hd + 1) * M_DH)
        ch = c_act[:, sl].astype(bf)
        q = jnp.dot(ch, wq_ref[hd], preferred_element_type=f32)
        kt = (jnp.dot(ch, wk_ref[hd], preferred_element_type=f32) * (M_DH ** -0.5)).T
        mq_s[:, sl] = q.astype(bf)
        mv_s[:, sl] = jnp.dot(zml_s[:, sl].astype(bf), wv_ref[hd], preferred_element_type=f32).astype(bf)
        for c in range(nc):
            ls = slice(c * CHUNK, (c + 1) * CHUNK)
            mkt_s[c * M_HEADS + hd] = kt[:, ls].astype(bf)
            mkw_s[c * M_HEADS + hd] = (kt[:, ls] * w_st[c][hd:hd + 1, :]).astype(bf)
            mqw_s[ls, sl] = (q[ls, :] * wi_cols[c][:, SUBLANES + hd:SUBLANES + hd + 1]).astype(bf)

    cos = cos_ref[...]
    sin = sin_ref[...]
    for c0, scale in ((0, 1.0), (R_QK, R_DK ** -0.5)):
        x1 = zret_s[:, c0:c0 + LANES]
        x2 = zret_s[:, c0 + LANES:c0 + 2 * LANES]
        zret_s[:, c0:c0 + LANES] = (x1 * cos - x2 * sin) * scale
        zret_s[:, c0 + LANES:c0 + 2 * LANES] = (x1 * sin + x2 * cos) * scale

    def log_gamma(head):
        lg = jnp.full(head.shape, math.log(1.0 - 2.0 ** -5.0), f32)
        for hd in range(1, R_HEADS):
            lg = jnp.where(head == hd, math.log(1.0 - 2.0 ** (-5.0 - hd)), lg)
        return lg

    def ret_head_of(idx):
        return (idx % LANES) // (R_DK // 2)

    n_last = float(min(CHUNK, nreal))
    lane_q = lax.broadcasted_iota(jnp.int32, (CHUNK, R_QK), 1)
    cnt_q = jnp.minimum(lax.broadcasted_iota(jnp.int32, (CHUNK, R_QK), 0) + 1, nreal).astype(f32)
    eq = jnp.exp(cnt_q * log_gamma(ret_head_of(lane_q)))
    krow = lax.broadcasted_iota(jnp.int32, (R_QK, CHUNK), 0)
    cnt_k = jnp.minimum(lax.broadcasted_iota(jnp.int32, (R_QK, CHUNK), 1) + 1, nreal).astype(f32)
    lg_k = log_gamma(ret_head_of(krow))
    ek_t = jnp.exp(-cnt_k * lg_k)
    est_t = jnp.exp((n_last - cnt_k) * lg_k)
    ret_rows = [ret_head_of(krow) == hd for hd in range(R_HEADS)]
    gla_rows = [(krow // G_DK) == hd for hd in range(G_HEADS)]
    srow = lax.broadcasted_iota(jnp.int32, (R_QK, R_V), 0)
    scol = lax.broadcasted_iota(jnp.int32, (R_QK, R_V), 1) // R_DV
    ret_blk = ret_head_of(srow) == scol
    gla_blk = (srow // G_DK) == scol
    dec_ret = jnp.exp(n_last * log_gamma(ret_head_of(srow)))

    for c in range(nc):
        rs = slice(c * CHUNK, (c + 1) * CHUNK)
        rq_s[rs, :] = (zret_s[rs, 0:R_QK] * eq).astype(bf)
        kt = zret_s[rs, R_QK:2 * R_QK].T
        kin = kt * ek_t
        for hd in range(R_HEADS):
            rkm_s[c * R_HEADS + hd] = jnp.where(ret_rows[hd], kin, 0.0).astype(bf)
        rkst_s[c] = (kt * est_t).astype(bf)
        v = zret_s[rs, 2 * R_QK:2 * R_QK + R_V].astype(bf)
        rv_s[rs, :] = v
        for p in range(2):
            rvbd_s[c * 2 + p] = _block_diag(v[:, 2 * p * R_DV:(2 * p + 1) * R_DV],
                                            v[:, (2 * p + 1) * R_DV:(2 * p + 2) * R_DV])

    la = _log_sigmoid(_dot(zs, wa2_ref[...]) + ba_ref[...]) * (1.0 / G_NORMALIZER)
    if masked:
        la = jnp.where(lax.broadcasted_iota(jnp.int32, la.shape, 0) < nreal, la, 0.0)
    for c in range(nc):
        rs = slice(c * CHUNK, (c + 1) * CHUNK)
        b = _cumsum_rows(la[rs, :], tri)
        b_t = b.T
        bl_t = b_t[:, CHUNK - 1:CHUNK]
        gq_s[rs, :] = (zgla_s[rs, 0:G_QK] * (G_DK ** -0.5) * jnp.exp(b)).astype(bf)
        kt = zgla_s[rs, G_QK:2 * G_QK].T
        kin = kt * jnp.exp(-b_t)
        for hd in range(G_HEADS):
            gkm_s[c * G_HEADS + hd] = jnp.where(gla_rows[hd], kin, 0.0).astype(bf)
        gkst_s[c] = (kt * jnp.exp(bl_t - b_t)).astype(bf)
        gdec_s[c] = jnp.broadcast_to(jnp.exp(bl_t), (G_QK, LANES))
        v = zgla_s[rs, 2 * G_QK:2 * G_QK + G_V].astype(bf)
        gv_s[rs, :] = v
        for p in range(2):
            gvbd_s[c * 2 + p] = _block_diag(v[:, 2 * p * G_DV:(2 * p + 1) * G_DV],
                                            v[:, (2 * p + 1) * G_DV:(2 * p + 2) * G_DV])

    ones_col = (lax.broadcasted_iota(jnp.int32, (CHUNK, LANES), 1) == 0).astype(bf)

    def chunk_body(c, carry):
        rows = pl.ds(pl.multiple_of(c * CHUNK, CHUNK), CHUNK)

        outs = _linear_attn_chunk(c, rows, rq_s, rkm_s, rkst_s, rv_s, rvbd_s, dec_ret, sret_s,
                                  ret_blk, causal2)
        for hd in range(R_HEADS):
            o_h = outs[hd // 2][:, (hd % 2) * R_DV:(hd % 2 + 1) * R_DV]
            gate = _silu(zret_s[rows, 2 * R_QK + R_V + hd * R_DV:2 * R_QK + R_V + (hd + 1) * R_DV])
            o_s[rows, hd * R_DV:(hd + 1) * R_DV] = (_head_norm(o_h) * gate).astype(bf)

        dec_g = jnp.concatenate([gdec_s[c]] * G_HEADS, axis=1)
        outs = _linear_attn_chunk(c, rows, gq_s, gkm_s, gkst_s, gv_s, gvbd_s, dec_g, sgla_s,
                                  gla_blk, causal2)
        for hd in range(G_HEADS):
            o_h = outs[hd // 2][:, (hd % 2) * G_DV:(hd % 2 + 1) * G_DV]
            gate = _silu(zgla_s[rows, 2 * G_QK + G_V + hd * G_DV:2 * G_QK + G_V + (hd + 1) * G_DV])
            o_s[rows, R_V + M_W + hd * G_DV:R_V + M_W + (hd + 1) * G_DV] = (
                _head_norm(o_h) * gate).astype(bf)

        cols = cols_s[c]
        s_old = sold_s[c]
        s_new = snew_s[c]
        for hd in range(M_HEADS):
            sl = slice(hd * M_DH, (hd + 1) * M_DH)
            sc = jnp.dot(mq_s[rows, sl], mkt_s[c * M_HEADS + hd], preferred_element_type=f32)
            sc = sc * dmat_s[c * M_HEADS + hd]
            cext = cext_s[hd]
            inter = jnp.dot(mqw_s[rows, sl], cext.astype(bf), preferred_element_type=f32)
            vh = mv_s[rows, sl]
            num = jnp.dot(sc.astype(bf), vh, preferred_element_type=f32) + inter[:, 0:M_DH]
            den = jnp.sum(sc, axis=1, keepdims=True) + inter[:, M_DH:M_DH + 1]
            hh = num / jnp.maximum(jnp.abs(den), cols[:, 2 * SUBLANES + hd:2 * SUBLANES + hd + 1])
            vext = jnp.concatenate([vh, ones_col], axis=1)
            dcext = jnp.dot(mkw_s[c * M_HEADS + hd], vext, preferred_element_type=f32)
            so = jnp.concatenate([s_old[hd:hd + 1, :]] * 2, axis=1)
            sn = jnp.concatenate([s_new[hd:hd + 1, :]] * 2, axis=1)
            cext_s[hd] = so * cext + sn * dcext
            o_m = _sigmoid(zml_s[rows, M_W + hd * M_DH:M_W + (hd + 1) * M_DH]) * (
                _head_norm(hh) + skip_ref[:, sl] * c_s[rows, sl])
            o_s[rows, R_V + hd * M_DH:R_V + (hd + 1) * M_DH] = o_m.astype(bf)
        return carry

    lax.fori_loop(0, nc, chunk_body, 0, unroll=True)

    mix = None
    for br, (c0, width) in enumerate(((0, R_V), (R_V, M_W), (R_V + M_W, G_V))):
        p = jnp.dot(o_s[:, c0:c0 + width], wbr_ref[c0:c0 + width, :], preferred_element_type=f32)
        mix = gates[br] * p if mix is None else mix + gates[br] * p
    mix_ref[0] = mix.astype(mix_ref.dtype)

    xc_s[0:SUBLANES, :] = xc_s[tb:tb + SUBLANES, :]

    @pl.when(pl.program_id(1) == pl.num_programs(1) - 1)
    def _():
        for hd in range(R_HEADS):
            cs = slice(hd * R_DV, (hd + 1) * R_DV)
            for half in range(2):
                sret_o[hd, half * (R_DK // 2):(half + 1) * (R_DK // 2), :] = sret_s[_ret_rows(hd, half), cs]
            sgla_o[hd] = sgla_s[hd * G_DK:(hd + 1) * G_DK, cs]
            c_o[hd] = cext_s[hd, :, 0:M_DH]
            n_o[hd:hd + 1, :] = cext_s[hd, :, M_DH:2 * M_DH].T[0:1, :]
        m_o[...] = m_s[...]
        conv_o[...] = xc_s[nreal + SUBLANES - (CONV_W - 1):nreal + SUBLANES, :]


def _layer_spec(arr, layer, grid_rank):
    shape = arr.shape[1:]
    zeros = (0,) * len(shape)
    if grid_rank == 1:
        return pl.BlockSpec((None,) + shape, lambda i: (layer,) + zeros, pipeline_mode=pl.Buffered(1))
    return pl.BlockSpec((None,) + shape, lambda b, i: (layer,) + zeros, pipeline_mode=pl.Buffered(1))


def _mixer_call(x, cos, sin, init, w, layer, *, nreal, tb):
    bsz, t, _ = x.shape
    nt = t // tb
    nc = tb // CHUNK

    def init_spec(arr, init_layer):
        shape = arr.shape[2:]
        zeros = (0,) * len(shape)
        if arr.shape[1] == 1:
            return pl.BlockSpec((None, None) + shape, lambda b, i: (init_layer, 0) + zeros)
        return pl.BlockSpec((None, None) + shape, lambda b, i: (init_layer, b) + zeros)

    def out_state(shape):
        zeros = (0,) * len(shape)
        return (jax.ShapeDtypeStruct((bsz,) + shape, jnp.float32),
                pl.BlockSpec((None,) + shape, lambda b, i: (b,) + zeros))

    def win_spec(k):
        return pl.BlockSpec((None, D_MODEL, R_V), lambda b, i: (layer, 0, k), pipeline_mode=pl.Buffered(1))

    assert IN_WIDTHS[:6] == (R_V // 2, R_V // 2, R_V, R_V, R_V, R_V)
    tail = (w["wsmall"], w["wgla"], w["wa2"], w["ba"], w["gate_b"], w["conv_w"], w["conv_b"],
            w["wq"], w["wk"], w["wv"], w["skip"], w["wg"], w["wbr"])
    weights = (w["g1"], w["wrqk"]) + (w["win"],) * 4 + tail
    weight_specs = ([_layer_spec(w["g1"], layer, 2), _layer_spec(w["wrqk"], layer, 2)]
                    + [win_spec(k) for k in range(1, 5)] + [_layer_spec(a, layer, 2) for a in tail])
    init_arrays, init_specs = (), []
    if init is not None:
        init_arrays = tuple(init[0])
        init_specs = [init_spec(a, init[1]) for a in init_arrays]
    outs = [(jax.ShapeDtypeStruct((bsz, t, D_MODEL), MXU_DTYPE),
             pl.BlockSpec((1, tb, D_MODEL), lambda b, i: (b, i, 0))),
            out_state((R_HEADS, R_DK, R_DV)), out_state((M_HEADS, M_DH, M_DH)), out_state((M_HEADS, M_DH)),
            out_state((SUBLANES, LANES)), out_state((CONV_W - 1, M_W)), out_state((G_HEADS, G_DK, G_DV))]
    f32, bf = jnp.float32, MXU_DTYPE
    lin_attn = [pltpu.VMEM((tb, R_QK), bf), pltpu.VMEM((nc * 4, R_QK, CHUNK), bf),
                pltpu.VMEM((nc, R_QK, CHUNK), bf), pltpu.VMEM((tb, R_V), bf),
                pltpu.VMEM((nc * 2, 2 * R_DV, 2 * R_DV), bf)]
    scratch = ([pltpu.VMEM((tb, 2 * R_QK + 2 * R_V), f32), pltpu.VMEM((tb, 2 * M_W), f32),
                pltpu.VMEM((tb, 2 * G_QK + 2 * G_V), f32), pltpu.VMEM((tb + SUBLANES, M_W), f32),
                pltpu.VMEM((tb, M_W), f32)]
               + lin_attn + lin_attn + [pltpu.VMEM((nc, G_QK, LANES), f32)]
               + [pltpu.VMEM((tb, M_W), bf), pltpu.VMEM((tb, M_W), bf),
                  pltpu.VMEM((nc * M_HEADS, M_DH, CHUNK), bf), pltpu.VMEM((nc * M_HEADS, M_DH, CHUNK), bf),
                  pltpu.VMEM((tb, M_W), bf), pltpu.VMEM((nc * M_HEADS, CHUNK, CHUNK), f32),
                  pltpu.VMEM((nc, CHUNK, LANES), f32), pltpu.VMEM((nc, SUBLANES, LANES), f32),
                  pltpu.VMEM((nc, SUBLANES, LANES), f32)]
               + [pltpu.VMEM((R_QK, R_V), f32), pltpu.VMEM((M_HEADS, M_DH, 2 * M_DH), f32),
                  pltpu.VMEM((SUBLANES, LANES), f32), pltpu.VMEM((G_QK, G_V), f32),
                  pltpu.VMEM((M_DH, LANES), f32)]
               + [pltpu.VMEM((tb, R_V + M_W + G_V), bf)])
    return pl.pallas_call(
        functools.partial(_mixer_kernel, nreal, tb, init is None),
        grid=(bsz, nt),
        in_specs=[pl.BlockSpec((1, tb, D_MODEL), lambda b, i: (b, i, 0)),
                  pl.BlockSpec((tb, LANES), lambda b, i: (i, 0)),
                  pl.BlockSpec((tb, LANES), lambda b, i: (i, 0))]
                 + init_specs + weight_specs,
        out_specs=[o[1] for o in outs],
        out_shape=[o[0] for o in outs],
        scratch_shapes=scratch,
        compiler_params=pltpu.CompilerParams(
            dimension_semantics=("arbitrary", "arbitrary"), vmem_limit_bytes=VMEM_LIMIT_BYTES),
        name="mixer",
    )(x, cos, sin, *init_arrays, *weights)


def _channel_kernel(final, x_ref, mix_ref, wout_ref, g2_ref, wfi_ref, wfo_ref, gf_ref, y_ref):
    f32 = jnp.float32
    x = x_ref[...] + jnp.dot(mix_ref[...], wout_ref[...], preferred_element_type=f32)
    h2 = _rmsnorm(x, g2_ref[...]).astype(MXU_DTYPE)
    acc = x
    for j in range(D_FF // _FF_TILE):
        cs = slice(j * _FF_TILE, (j + 1) * _FF_TILE)
        ug = jnp.dot(h2, wfi_ref[:, cs], preferred_element_type=f32)
        uv = jnp.dot(h2, wfi_ref[:, D_FF + j * _FF_TILE:D_FF + (j + 1) * _FF_TILE], preferred_element_type=f32)
        acc = acc + _dot(_silu(ug) * uv, wfo_ref[cs, :])
    y_ref[...] = _rmsnorm(acc, gf_ref[...]) if final else acc


def _channel_call(x2d, o2d, w, layer, norm_f, *, final, tm):
    n = x2d.shape[0]
    weights = (w["wout"], w["g2"], w["wfi"], w["wfo"])
    return pl.pallas_call(
        functools.partial(_channel_kernel, final),
        grid=(n // tm,),
        in_specs=[pl.BlockSpec((tm, D_MODEL), lambda i: (i, 0)),
                  pl.BlockSpec((tm, o2d.shape[1]), lambda i: (i, 0))]
                 + [_layer_spec(a, layer, 1) for a in weights]
                 + [pl.BlockSpec(norm_f.shape, lambda i: (0, 0), pipeline_mode=pl.Buffered(1))],
        out_specs=pl.BlockSpec((tm, D_MODEL), lambda i: (i, 0)),
        out_shape=jax.ShapeDtypeStruct((n, D_MODEL), jnp.float32),
        compiler_params=pltpu.CompilerParams(
            dimension_semantics=("arbitrary",), vmem_limit_bytes=VMEM_LIMIT_BYTES),
        name="channel",
    )(x2d, o2d, *weights, norm_f)


def _prep_weights(norm1, w_in, b_i, b_f, conv_w, conv_b, w_mq, w_mk, w_mv, m_skip, w_a2, b_a,
                  w_br_ret, w_br_mlstm, w_br_gla, w_out, norm2, w_ffn_in, w_ffn_out):
    f32, bf = jnp.float32, MXU_DTYPE
    depth = w_in.shape[0]
    offs = np.cumsum((0,) + IN_WIDTHS)
    win = w_in.astype(bf)
    rq, rk, rv, rg, mx, mz, mi, mf, gq, gk, gv, gr, ga, z_ret, z_ml, z_gla = (
        win[:, :, offs[i]:offs[i + 1]] for i in range(len(IN_WIDTHS)))

    def rotary_layout(wc):
        wc = wc.reshape(depth, D_MODEL, R_HEADS, 2, R_DK // 2)
        return jnp.swapaxes(wc, 2, 3).reshape(depth, D_MODEL, R_QK)

    def zcols(n):
        return jnp.zeros((depth, D_MODEL, n), bf)

    wsmall = jnp.concatenate([mi, zcols(SUBLANES - M_HEADS), mf, zcols(SUBLANES - M_HEADS), ga,
                              zcols(LANES - 2 * SUBLANES - G_RANK)], axis=2)
    wa2 = jnp.pad(w_a2, ((0, 0), (2 * SUBLANES, LANES - 2 * SUBLANES - G_RANK), (0, 0))).astype(bf)

    def gate_rows(b):
        return jnp.pad(jnp.broadcast_to(b.astype(f32)[:, :, None], (depth, M_HEADS, LANES)),
                       ((0, 0), (0, SUBLANES - M_HEADS), (0, 0)))

    return dict(
        g1=norm1[:, None, :], win=win, wrqk=jnp.concatenate([rotary_layout(rq), rotary_layout(rk)], axis=2),
        wsmall=wsmall, wgla=jnp.concatenate([gq, gk, gv, gr], axis=2), wa2=wa2, ba=b_a[:, None, :],
        gate_b=jnp.concatenate([gate_rows(b_i), gate_rows(b_f)], axis=1),
        conv_w=conv_w, conv_b=conv_b[:, None, :], wq=w_mq.astype(bf), wk=w_mk.astype(bf), wv=w_mv.astype(bf),
        skip=m_skip[:, None, :],
        wg=jnp.concatenate([z_ret, z_ml, z_gla], axis=2),
        wbr=jnp.concatenate([w_br_ret, w_br_mlstm, w_br_gla], axis=1).astype(bf),
        wout=w_out.astype(bf), g2=norm2[:, None, :], wfi=w_ffn_in.astype(bf), wfo=w_ffn_out.astype(bf))


def _rotary_tables(pos):
    half = R_DK // 2
    inv = 1.0 / (ROPE_BASE ** jnp.linspace(0.0, 1.0, half, dtype=jnp.float32))
    ang = pos[:, None] * inv[None, :]
    return jnp.tile(jnp.cos(ang), (1, R_HEADS)), jnp.tile(jnp.sin(ang), (1, R_HEADS))


def _m_rows(m):
    pad = [(0, 0)] * (m.ndim - 1) + [(0, SUBLANES - M_HEADS), (0, 0)]
    return jnp.pad(jnp.broadcast_to(m.astype(jnp.float32)[..., None], m.shape + (LANES,)), pad)


def _trunk(x, pos0, inits, w, norm_f, *, tb, tm):
    bsz, n, _ = x.shape
    t = -(-n // CHUNK) * CHUNK
    rows = bsz * n
    cos, sin = _rotary_tables(pos0 + jnp.arange(t, dtype=jnp.float32))
    states = []
    for l in range(DEPTH):
        xp = x if t == n else jnp.pad(x, ((0, 0), (0, t - n), (0, 0)))
        res = _mixer_call(xp, cos, sin, inits[l], w, l, nreal=n, tb=min(tb, t))
        states.append(tuple(res[1:]))
        o = res[0] if t == n else res[0][:, 0:n]
        x = _channel_call(x.reshape(rows, D_MODEL), o.reshape(rows, -1), w, l, norm_f,
                          final=(l == DEPTH - 1), tm=min(tm, rows)).reshape(bsz, n, D_MODEL)
    return x, states


def kernel(x_prompt, x_sample, state_ret, state_mlstm_c, state_mlstm_n, state_mlstm_m, state_mlstm_conv, state_gla, meta_tokens, norm1, w_in, b_mlstm_i, b_mlstm_f, conv_w, conv_b, w_mq, w_mk, w_mv, m_skip, w_gla_a2, b_gla_a, w_br_ret, w_br_mlstm, w_br_gla, w_out, norm2, w_ffn_in, w_ffn_out, norm_f):
    f32 = jnp.float32
    w = _prep_weights(norm1, w_in, b_mlstm_i, b_mlstm_f, conv_w, conv_b, w_mq, w_mk, w_mv, m_skip,
                      w_gla_a2, b_gla_a, w_br_ret, w_br_mlstm, w_br_gla, w_out, norm2, w_ffn_in, w_ffn_out)
    gf = norm_f[None]
    tiles = dict(tb=MIXER_BLOCK, tm=CHANNEL_TILE)

    _, meta_states = _trunk(meta_tokens.astype(f32)[None], 0.0, [None] * DEPTH, w, gf, **tiles)
    p_init = [(tuple(a[None] for a in st), 0) for st in meta_states]
    yp, p_states = _trunk(x_prompt, float(N_META), p_init, w, gf, **tiles)

    s_arrays = (state_ret, state_mlstm_c, state_mlstm_n, _m_rows(state_mlstm_m), state_mlstm_conv, state_gla)
    ys, s_states = _trunk(x_sample, float(N_META + PAST_LEN), [(s_arrays, l) for l in range(DEPTH)], w, gf,
                          **tiles)

    def stack(states):
        ret, c, n, m, conv, gla = (jnp.stack([st[i] for st in states]) for i in range(6))
        return ret, c, n, m[:, :, 0:M_HEADS, 0], conv, gla

    return (yp, ys) + stack(p_states) + stack(s_states)
```

```python
import functools
import math

import jax
import jax.numpy as jnp
import numpy as np
from jax import lax
from jax.experimental import pallas as pl
from jax.experimental.pallas import tpu as pltpu

D_MODEL = 1024
DEPTH = 2
N_META = 16
PAST_LEN = 2048
EPS = 1e-6
R_HEADS, R_DK, R_DV = 4, 64, 128
ROPE_BASE = 10000.0
M_HEADS, M_DH, CONV_W = 4, 128, 4
G_HEADS, G_DK, G_DV, G_RANK = 4, 64, 128, 16
G_NORMALIZER = 16.0
R_QK, R_V = R_HEADS * R_DK, R_HEADS * R_DV
M_W = M_HEADS * M_DH
G_QK, G_V = G_HEADS * G_DK, G_HEADS * G_DV
D_FF = ((-(-8 * D_MODEL // 3)) + 255) // 256 * 256
IN_WIDTHS = (R_QK, R_QK, R_V, R_V, M_W, M_W, M_HEADS, M_HEADS,
             G_QK, G_QK, G_V, G_V, G_RANK, D_MODEL, D_MODEL, D_MODEL)

LANES = 128
SUBLANES = 8
CHUNK = LANES
MXU_DTYPE = jnp.bfloat16
VMEM_LIMIT_BYTES = 56 * 1024 * 1024
MIXER_BLOCK = 512
CHANNEL_TILE = 512

_FF_TILE = 256


def _dot(a, b):
    return jnp.dot(a.astype(MXU_DTYPE), b.astype(MXU_DTYPE), preferred_element_type=jnp.float32)


def _rmsnorm(x, g):
    return x * lax.rsqrt(jnp.mean(x * x, axis=-1, keepdims=True) + EPS) * g


def _head_norm(o):
    return o * lax.rsqrt(jnp.mean(o * o, axis=-1, keepdims=True) + EPS)


def _sigmoid(x):
    return 1.0 / (1.0 + jnp.exp(-x))


def _silu(x):
    return x * _sigmoid(x)


def _log_sigmoid(x):
    return -(jnp.maximum(-x, 0.0) + jnp.log(1.0 + jnp.exp(-jnp.abs(x))))


def _cumsum_rows(x, tri):
    hi = x.astype(MXU_DTYPE)
    r1 = x - hi.astype(jnp.float32)
    mid = r1.astype(MXU_DTYPE)
    lo = (r1 - mid.astype(jnp.float32)).astype(MXU_DTYPE)
    f = functools.partial(jnp.dot, preferred_element_type=jnp.float32)
    return f(tri, hi) + f(tri, mid) + f(tri, lo)


def _block_diag(v0, v1):
    z = jnp.zeros_like(v0)
    return jnp.concatenate([jnp.concatenate([v0, z], axis=1), jnp.concatenate([z, v1], axis=1)], axis=0)


def _linear_attn_chunk(c, rows, q_s, km_s, kst_s, v_s, vbd_s, dec, s_ref, blk_mask, causal2):
    q_in = q_s[rows, :]
    s_prev = s_ref[...]
    o_inter = jnp.dot(q_in, s_prev.astype(MXU_DTYPE), preferred_element_type=jnp.float32)
    outs = []
    for p in range(2):
        kt = jnp.concatenate([km_s[c * 4 + 2 * p], km_s[c * 4 + 2 * p + 1]], axis=1)
        sc = jnp.dot(q_in, kt, preferred_element_type=jnp.float32)
        sc = jnp.where(causal2, sc, 0.0).astype(MXU_DTYPE)
        outs.append(jnp.dot(sc, vbd_s[c * 2 + p], preferred_element_type=jnp.float32)
                    + o_inter[:, p * 2 * LANES:(p + 1) * 2 * LANES])
    ds = jnp.dot(kst_s[c], v_s[rows, :], preferred_element_type=jnp.float32)
    s_ref[...] = dec * s_prev + jnp.where(blk_mask, ds, 0.0)
    return outs


def _ret_rows(hd, half):
    start = half * LANES + hd * (R_DK // 2)
    return slice(start, start + R_DK // 2)


def _mixer_kernel(nreal, tb, zero_init, *refs):
    n_init = 0 if zero_init else 6
    x_ref, cos_ref, sin_ref = refs[0:3]
    init = refs[3:3 + n_init]
    (g1_ref, wrqk_ref, wrv_ref, wrg_ref, wmx_ref, wmz_ref, wsm_ref, wgla_ref, wa2_ref, ba_ref, gb_ref, cw_ref, cb_ref, wq_ref, wk_ref, wv_ref, skip_ref,
     o_ref, sret_o, c_o, n_o, m_o, conv_o, sgla_o,
     zret_s, zml_s, zgla_s, xc_s, c_s,
     rq_s, rkm_s, rkst_s, rv_s, rvbd_s,
     gq_s, gkm_s, gkst_s, gv_s, gvbd_s, gdec_s,
     mq_s, mqw_s, mkt_s, mkw_s, mv_s, dmat_s, cols_s, sold_s, snew_s,
     sret_s, cext_s, m_s, sgla_s, tmp_s) = refs[3 + n_init:]
    f32 = jnp.float32
    bf = MXU_DTYPE
    nc = tb // CHUNK
    nreal = min(nreal, tb)
    masked = nreal < tb
    lane0 = lax.broadcasted_iota(jnp.int32, (M_DH, LANES), 1) == 0

    @pl.when(pl.program_id(1) == 0)
    def _():
        sret_s[...] = jnp.zeros(sret_s.shape, f32)
        sgla_s[...] = jnp.zeros(sgla_s.shape, f32)
        xc_s[0:SUBLANES, :] = jnp.zeros((SUBLANES, M_W), f32)
        if zero_init:
            cext_s[...] = jnp.zeros(cext_s.shape, f32)
            m_s[...] = jnp.zeros(m_s.shape, f32)
        else:
            sret0, c0, n0, m0, conv0, sgla0 = init
            tmp_s[...] = jnp.zeros(tmp_s.shape, f32)
            tmp_s[0:M_HEADS, :] = n0[...]
            n_cols = tmp_s[...].T
            for hd in range(R_HEADS):
                cs = slice(hd * R_DV, (hd + 1) * R_DV)
                for half in range(2):
                    sret_s[_ret_rows(hd, half), cs] = sret0[hd, half * (R_DK // 2):(half + 1) * (R_DK // 2), :]
                sgla_s[hd * G_DK:(hd + 1) * G_DK, cs] = sgla0[hd]
                cext_s[hd, :, 0:M_DH] = c0[hd]
                cext_s[hd, :, M_DH:2 * M_DH] = jnp.where(lane0, n_cols[:, hd:hd + 1], 0.0)
            m_s[...] = m0[...]
            xc_s[SUBLANES - (CONV_W - 1):SUBLANES, :] = conv0[...]

    h = _rmsnorm(x_ref[0], g1_ref[...])
    if masked:
        h = jnp.where(lax.broadcasted_iota(jnp.int32, h.shape, 0) < nreal, h, 0.0)
    h = h.astype(bf)
    zs = jnp.dot(h, wsm_ref[...], preferred_element_type=f32)
    zml_s[:, 0:M_W] = jnp.dot(h, wmx_ref[...], preferred_element_type=f32)
    zml_s[:, M_W:2 * M_W] = jnp.dot(h, wmz_ref[...], preferred_element_type=f32)
    zret_s[:, 0:2 * R_QK] = jnp.dot(h, wrqk_ref[...], preferred_element_type=f32)
    zret_s[:, 2 * R_QK:2 * R_QK + R_V] = jnp.dot(h, wrv_ref[...], preferred_element_type=f32)
    zret_s[:, 2 * R_QK + R_V:2 * R_QK + 2 * R_V] = jnp.dot(h, wrg_ref[...], preferred_element_type=f32)
    zgla_s[...] = jnp.dot(h, wgla_ref[...], preferred_element_type=f32)

    row = lax.broadcasted_iota(jnp.int32, (CHUNK, CHUNK), 0)
    col = lax.broadcasted_iota(jnp.int32, (CHUNK, CHUNK), 1)
    causal = row >= col
    causal2 = jnp.concatenate([causal, causal], axis=1)
    tri = causal.astype(bf)

    zst = zs.T
    ig_all = zst[0:SUBLANES, :] + gb_ref[0:SUBLANES, 0:1]
    lf_all = _log_sigmoid(zst[SUBLANES:2 * SUBLANES, :] + gb_ref[SUBLANES:2 * SUBLANES, 0:1])
    tok = lax.broadcasted_iota(jnp.int32, ig_all.shape, 1)
    if masked:
        ig_all = jnp.where(tok < nreal, ig_all, -jnp.inf)
        lf_all = jnp.where(tok < nreal, lf_all, 0.0)
    in_chunk = tok % CHUNK

    def seg_scan(x, op, fill):
        s = 1
        while s < CHUNK:
            x = op(x, jnp.where(in_chunk >= s, pltpu.roll(x, s, axis=1), fill))
            s *= 2
        return x

    b_all = seg_scan(lf_all, jnp.add, 0.0)
    a_all = ig_all - b_all
    g_all = seg_scan(a_all, jnp.maximum, -jnp.inf)
    m_p = m_s[:, 0:1]
    w_st, wi_cols = [], []
    for c in range(nc):
        ls = slice(c * CHUNK, (c + 1) * CHUNK)
        b_c, a_c, g_c = b_all[:, ls], a_all[:, ls], g_all[:, ls]
        blast = jnp.sum(lf_all[:, ls], axis=1, keepdims=True)
        mloc = blast + jnp.max(a_c, axis=1, keepdims=True)
        m_new = jnp.maximum(blast + m_p, mloc)
        sold_s[c] = jnp.broadcast_to(jnp.exp(blast + m_p - m_new), (SUBLANES, LANES))
        snew_s[c] = jnp.broadcast_to(jnp.exp(mloc - m_new), (SUBLANES, LANES))
        w_st.append(jnp.exp(a_c + (blast - mloc)))
        m_t = b_c + jnp.maximum(m_p, g_c)
        stats = jnp.concatenate([b_c - m_t, jnp.exp(b_c + m_p - m_t), jnp.exp(-m_t),
                                 jnp.zeros((LANES - 3 * SUBLANES, LANES), f32)], axis=0)
        cols = stats.T
        cols_s[c] = cols
        wi_cols.append(cols)
        for hd in range(M_HEADS):
            dmat_s[c * M_HEADS + hd] = jnp.exp(
                jnp.where(causal, cols[:, hd:hd + 1] + a_c[hd:hd + 1, :], -jnp.inf))
        m_p = m_new
    m_s[...] = jnp.broadcast_to(m_p, (SUBLANES, LANES))

    xc_s[SUBLANES:SUBLANES + tb, :] = zml_s[:, 0:M_W]
    cpre = cb_ref[...] + sum(
        xc_s[SUBLANES - (CONV_W - 1) + j:SUBLANES - (CONV_W - 1) + j + tb, :] * cw_ref[j:j + 1, :]
        for j in range(CONV_W))
    c_act = _silu(cpre)
    c_s[...] = c_act
    for hd in range(M_HEADS):
        sl = slice(hd * M_DH, (hd + 1) * M_DH)
        ch = c_act[:, sl].astype(bf)
        q = jnp.dot(ch, wq_ref[hd], preferred_element_type=f32)
        kt = (jnp.dot(ch, wk_ref[hd], preferred_element_type=f32) * (M_DH ** -0.5)).T
        mq_s[:, sl] = q.astype(bf)
        mv_s[:, sl] = jnp.dot(zml_s[:, sl].astype(bf), wv_ref[hd], preferred_element_type=f32).astype(bf)
        for c in range(nc):
            ls = slice(c * CHUNK, (c + 1) * CHUNK)
            mkt_s[c * M_HEADS + hd] = kt[:, ls].astype(bf)
            mkw_s[c * M_HEADS + hd] = (kt[:, ls] * w_st[c][hd:hd + 1, :]).astype(bf)
            mqw_s[ls, sl] = (q[ls, :] * wi_cols[c][:, SUBLANES + hd:SUBLANES + hd + 1]).astype(bf)

    cos = cos_ref[...]
    sin = sin_ref[...]
    for c0, scale in ((0, 1.0), (R_QK, R_DK ** -0.5)):
        x1 = zret_s[:, c0:c0 + LANES]
        x2 = zret_s[:, c0 + LANES:c0 + 2 * LANES]
        zret_s[:, c0:c0 + LANES] = (x1 * cos - x2 * sin) * scale
        zret_s[:, c0 + LANES:c0 + 2 * LANES] = (x1 * sin + x2 * cos) * scale

    def log_gamma(head):
        lg = jnp.full(head.shape, math.log(1.0 - 2.0 ** -5.0), f32)
        for hd in range(1, R_HEADS):
            lg = jnp.where(head == hd, math.log(1.0 - 2.0 ** (-5.0 - hd)), lg)
        return lg

    def ret_head_of(idx):
        return (idx % LANES) // (R_DK // 2)

    n_last = float(min(CHUNK, nreal))
    lane_q = lax.broadcasted_iota(jnp.int32, (CHUNK, R_QK), 1)
    cnt_q = jnp.minimum(lax.broadcasted_iota(jnp.int32, (CHUNK, R_QK), 0) + 1, nreal).astype(f32)
    eq = jnp.exp(cnt_q * log_gamma(ret_head_of(lane_q)))
    krow = lax.broadcasted_iota(jnp.int32, (R_QK, CHUNK), 0)
    cnt_k = jnp.minimum(lax.broadcasted_iota(jnp.int32, (R_QK, CHUNK), 1) + 1, nreal).astype(f32)
    lg_k = log_gamma(ret_head_of(krow))
    ek_t = jnp.exp(-cnt_k * lg_k)
    est_t = jnp.exp((n_last - cnt_k) * lg_k)
    ret_rows = [ret_head_of(krow) == hd for hd in range(R_HEADS)]
    gla_rows = [(krow // G_DK) == hd for hd in range(G_HEADS)]
    srow = lax.broadcasted_iota(jnp.int32, (R_QK, R_V), 0)
    scol = lax.broadcasted_iota(jnp.int32, (R_QK, R_V), 1) // R_DV
    ret_blk = ret_head_of(srow) == scol
    gla_blk = (srow // G_DK) == scol
    dec_ret = jnp.exp(n_last * log_gamma(ret_head_of(srow)))

    for c in range(nc):
        rs = slice(c * CHUNK, (c + 1) * CHUNK)
        rq_s[rs, :] = (zret_s[rs, 0:R_QK] * eq).astype(bf)
        kt = zret_s[rs, R_QK:2 * R_QK].T
        kin = kt * ek_t
        for hd in range(R_HEADS):
            rkm_s[c * R_HEADS + hd] = jnp.where(ret_rows[hd], kin, 0.0).astype(bf)
        rkst_s[c] = (kt * est_t).astype(bf)
        v = zret_s[rs, 2 * R_QK:2 * R_QK + R_V].astype(bf)
        rv_s[rs, :] = v
        for p in range(2):
            rvbd_s[c * 2 + p] = _block_diag(v[:, 2 * p * R_DV:(2 * p + 1) * R_DV],
                                            v[:, (2 * p + 1) * R_DV:(2 * p + 2) * R_DV])

    la = _log_sigmoid(_dot(zs, wa2_ref[...]) + ba_ref[...]) * (1.0 / G_NORMALIZER)
    if masked:
        la = jnp.where(lax.broadcasted_iota(jnp.int32, la.shape, 0) < nreal, la, 0.0)
    for c in range(nc):
        rs = slice(c * CHUNK, (c + 1) * CHUNK)
        b = _cumsum_rows(la[rs, :], tri)
        b_t = b.T
        bl_t = b_t[:, CHUNK - 1:CHUNK]
        gq_s[rs, :] = (zgla_s[rs, 0:G_QK] * (G_DK ** -0.5) * jnp.exp(b)).astype(bf)
        kt = zgla_s[rs, G_QK:2 * G_QK].T
        kin = kt * jnp.exp(-b_t)
        for hd in range(G_HEADS):
            gkm_s[c * G_HEADS + hd] = jnp.where(gla_rows[hd], kin, 0.0).astype(bf)
        gkst_s[c] = (kt * jnp.exp(bl_t - b_t)).astype(bf)
        gdec_s[c] = jnp.broadcast_to(jnp.exp(bl_t), (G_QK, LANES))
        v = zgla_s[rs, 2 * G_QK:2 * G_QK + G_V].astype(bf)
        gv_s[rs, :] = v
        for p in range(2):
            gvbd_s[c * 2 + p] = _block_diag(v[:, 2 * p * G_DV:(2 * p + 1) * G_DV],
                                            v[:, (2 * p + 1) * G_DV:(2 * p + 2) * G_DV])

    ones_col = (lax.broadcasted_iota(jnp.int32, (CHUNK, LANES), 1) == 0).astype(bf)

    def chunk_body(c, carry):
        rows = pl.ds(pl.multiple_of(c * CHUNK, CHUNK), CHUNK)

        outs = _linear_attn_chunk(c, rows, rq_s, rkm_s, rkst_s, rv_s, rvbd_s, dec_ret, sret_s,
                                  ret_blk, causal2)
        for hd in range(R_HEADS):
            o_h = outs[hd // 2][:, (hd % 2) * R_DV:(hd % 2 + 1) * R_DV]
            gate = _silu(zret_s[rows, 2 * R_QK + R_V + hd * R_DV:2 * R_QK + R_V + (hd + 1) * R_DV])
            o_ref[0, rows, hd * R_DV:(hd + 1) * R_DV] = (_head_norm(o_h) * gate).astype(o_ref.dtype)

        dec_g = jnp.concatenate([gdec_s[c]] * G_HEADS, axis=1)
        outs = _linear_attn_chunk(c, rows, gq_s, gkm_s, gkst_s, gv_s, gvbd_s, dec_g, sgla_s,
                                  gla_blk, causal2)
        for hd in range(G_HEADS):
            o_h = outs[hd // 2][:, (hd % 2) * G_DV:(hd % 2 + 1) * G_DV]
            gate = _silu(zgla_s[rows, 2 * G_QK + G_V + hd * G_DV:2 * G_QK + G_V + (hd + 1) * G_DV])
            o_ref[0, rows, R_V + M_W + hd * G_DV:R_V + M_W + (hd + 1) * G_DV] = (
                _head_norm(o_h) * gate).astype(o_ref.dtype)

        cols = cols_s[c]
        s_old = sold_s[c]
        s_new = snew_s[c]
        for hd in range(M_HEADS):
            sl = slice(hd * M_DH, (hd + 1) * M_DH)
            sc = jnp.dot(mq_s[rows, sl], mkt_s[c * M_HEADS + hd], preferred_element_type=f32)
            sc = sc * dmat_s[c * M_HEADS + hd]
            cext = cext_s[hd]
            inter = jnp.dot(mqw_s[rows, sl], cext.astype(bf), preferred_element_type=f32)
            vh = mv_s[rows, sl]
            num = jnp.dot(sc.astype(bf), vh, preferred_element_type=f32) + inter[:, 0:M_DH]
            den = jnp.sum(sc, axis=1, keepdims=True) + inter[:, M_DH:M_DH + 1]
            hh = num / jnp.maximum(jnp.abs(den), cols[:, 2 * SUBLANES + hd:2 * SUBLANES + hd + 1])
            vext = jnp.concatenate([vh, ones_col], axis=1)
            dcext = jnp.dot(mkw_s[c * M_HEADS + hd], vext, preferred_element_type=f32)
            so = jnp.concatenate([s_old[hd:hd + 1, :]] * 2, axis=1)
            sn = jnp.concatenate([s_new[hd:hd + 1, :]] * 2, axis=1)
            cext_s[hd] = so * cext + sn * dcext
            o_m = _sigmoid(zml_s[rows, M_W + hd * M_DH:M_W + (hd + 1) * M_DH]) * (
                _head_norm(hh) + skip_ref[:, sl] * c_s[rows, sl])
            o_ref[0, rows, R_V + hd * M_DH:R_V + (hd + 1) * M_DH] = o_m.astype(o_ref.dtype)
        return carry

    lax.fori_loop(0, nc, chunk_body, 0, unroll=True)

    xc_s[0:SUBLANES, :] = xc_s[tb:tb + SUBLANES, :]

    @pl.when(pl.program_id(1) == pl.num_programs(1) - 1)
    def _():
        for hd in range(R_HEADS):
            cs = slice(hd * R_DV, (hd + 1) * R_DV)
            for half in range(2):
                sret_o[hd, half * (R_DK // 2):(half + 1) * (R_DK // 2), :] = sret_s[_ret_rows(hd, half), cs]
            sgla_o[hd] = sgla_s[hd * G_DK:(hd + 1) * G_DK, cs]
            c_o[hd] = cext_s[hd, :, 0:M_DH]
            n_o[hd:hd + 1, :] = cext_s[hd, :, M_DH:2 * M_DH].T[0:1, :]
        m_o[...] = m_s[...]
        conv_o[...] = xc_s[nreal + SUBLANES - (CONV_W - 1):nreal + SUBLANES, :]


def _layer_spec(arr, layer, grid_rank):
    shape = arr.shape[1:]
    zeros = (0,) * len(shape)
    if grid_rank == 1:
        return pl.BlockSpec((None,) + shape, lambda i: (layer,) + zeros, pipeline_mode=pl.Buffered(1))
    return pl.BlockSpec((None,) + shape, lambda b, i: (layer,) + zeros, pipeline_mode=pl.Buffered(1))


def _mixer_call(x, cos, sin, init, w, layer, *, nreal, tb):
    bsz, t, _ = x.shape
    nt = t // tb
    nc = tb // CHUNK

    def init_spec(arr, init_layer):
        shape = arr.shape[2:]
        zeros = (0,) * len(shape)
        if arr.shape[1] == 1:
            return pl.BlockSpec((None, None) + shape, lambda b, i: (init_layer, 0) + zeros)
        return pl.BlockSpec((None, None) + shape, lambda b, i: (init_layer, b) + zeros)

    def out_state(shape):
        zeros = (0,) * len(shape)
        return (jax.ShapeDtypeStruct((bsz,) + shape, jnp.float32),
                pl.BlockSpec((None,) + shape, lambda b, i: (b,) + zeros))

    def win_spec(k):
        return pl.BlockSpec((None, D_MODEL, R_V), lambda b, i: (layer, 0, k), pipeline_mode=pl.Buffered(1))

    assert IN_WIDTHS[:6] == (R_V // 2, R_V // 2, R_V, R_V, R_V, R_V)
    tail = (w["wsmall"], w["wgla"], w["wa2"], w["ba"], w["gate_b"], w["conv_w"], w["conv_b"],
            w["wq"], w["wk"], w["wv"], w["skip"])
    weights = (w["g1"], w["wrqk"]) + (w["win"],) * 4 + tail
    weight_specs = ([_layer_spec(w["g1"], layer, 2), _layer_spec(w["wrqk"], layer, 2)]
                    + [win_spec(k) for k in range(1, 5)] + [_layer_spec(a, layer, 2) for a in tail])
    init_arrays, init_specs = (), []
    if init is not None:
        init_arrays = tuple(init[0])
        init_specs = [init_spec(a, init[1]) for a in init_arrays]
    outs = [(jax.ShapeDtypeStruct((bsz, t, R_V + M_W + G_V), MXU_DTYPE),
             pl.BlockSpec((1, tb, R_V + M_W + G_V), lambda b, i: (b, i, 0))),
            out_state((R_HEADS, R_DK, R_DV)), out_state((M_HEADS, M_DH, M_DH)), out_state((M_HEADS, M_DH)),
            out_state((SUBLANES, LANES)), out_state((CONV_W - 1, M_W)), out_state((G_HEADS, G_DK, G_DV))]
    f32, bf = jnp.float32, MXU_DTYPE
    lin_attn = [pltpu.VMEM((tb, R_QK), bf), pltpu.VMEM((nc * 4, R_QK, CHUNK), bf),
                pltpu.VMEM((nc, R_QK, CHUNK), bf), pltpu.VMEM((tb, R_V), bf),
                pltpu.VMEM((nc * 2, 2 * R_DV, 2 * R_DV), bf)]
    scratch = ([pltpu.VMEM((tb, 2 * R_QK + 2 * R_V), f32), pltpu.VMEM((tb, 2 * M_W), f32),
                pltpu.VMEM((tb, 2 * G_QK + 2 * G_V), f32), pltpu.VMEM((tb + SUBLANES, M_W), f32),
                pltpu.VMEM((tb, M_W), f32)]
               + lin_attn + lin_attn + [pltpu.VMEM((nc, G_QK, LANES), f32)]
               + [pltpu.VMEM((tb, M_W), bf), pltpu.VMEM((tb, M_W), bf),
                  pltpu.VMEM((nc * M_HEADS, M_DH, CHUNK), bf), pltpu.VMEM((nc * M_HEADS, M_DH, CHUNK), bf),
                  pltpu.VMEM((tb, M_W), bf), pltpu.VMEM((nc * M_HEADS, CHUNK, CHUNK), f32),
                  pltpu.VMEM((nc, CHUNK, LANES), f32), pltpu.VMEM((nc, SUBLANES, LANES), f32),
                  pltpu.VMEM((nc, SUBLANES, LANES), f32)]
               + [pltpu.VMEM((R_QK, R_V), f32), pltpu.VMEM((M_HEADS, M_DH, 2 * M_DH), f32),
                  pltpu.VMEM((SUBLANES, LANES), f32), pltpu.VMEM((G_QK, G_V), f32),
                  pltpu.VMEM((M_DH, LANES), f32)])
    return pl.pallas_call(
        functools.partial(_mixer_kernel, nreal, tb, init is None),
        grid=(bsz, nt),
        in_specs=[pl.BlockSpec((1, tb, D_MODEL), lambda b, i: (b, i, 0)),
                  pl.BlockSpec((tb, LANES), lambda b, i: (i, 0)),
                  pl.BlockSpec((tb, LANES), lambda b, i: (i, 0))]
                 + init_specs + weight_specs,
        out_specs=[o[1] for o in outs],
        out_shape=[o[0] for o in outs],
        scratch_shapes=scratch,
        compiler_params=pltpu.CompilerParams(
            dimension_semantics=("arbitrary", "arbitrary"), vmem_limit_bytes=VMEM_LIMIT_BYTES),
        name="mixer",
    )(x, cos, sin, *init_arrays, *weights)


def _channel_kernel(final, x_ref, o_ref, g1_ref, wg_ref, wbr_ref, wout_ref, g2_ref, wfi_ref, wfo_ref,
                    gf_ref, y_ref):
    f32 = jnp.float32
    x = x_ref[...]
    h = _rmsnorm(x, g1_ref[...]).astype(MXU_DTYPE)
    mix = None
    for br, (c0, width) in enumerate(((0, R_V), (R_V, M_W), (R_V + M_W, G_V))):
        gate = _sigmoid(jnp.dot(h, wg_ref[:, br * D_MODEL:(br + 1) * D_MODEL], preferred_element_type=f32))
        p = jnp.dot(o_ref[:, c0:c0 + width], wbr_ref[c0:c0 + width, :], preferred_element_type=f32)
        mix = gate * p if mix is None else mix + gate * p
    x = x + _dot(mix, wout_ref[...])
    h2 = _rmsnorm(x, g2_ref[...]).astype(MXU_DTYPE)
    acc = x
    for j in range(D_FF // _FF_TILE):
        cs = slice(j * _FF_TILE, (j + 1) * _FF_TILE)
        ug = jnp.dot(h2, wfi_ref[:, cs], preferred_element_type=f32)
        uv = jnp.dot(h2, wfi_ref[:, D_FF + j * _FF_TILE:D_FF + (j + 1) * _FF_TILE], preferred_element_type=f32)
        acc = acc + _dot(_silu(ug) * uv, wfo_ref[cs, :])
    y_ref[...] = _rmsnorm(acc, gf_ref[...]) if final else acc


def _channel_call(x2d, o2d, w, layer, norm_f, *, final, tm):
    n = x2d.shape[0]
    weights = (w["g1"], w["wg"], w["wbr"], w["wout"], w["g2"], w["wfi"], w["wfo"])
    return pl.pallas_call(
        functools.partial(_channel_kernel, final),
        grid=(n // tm,),
        in_specs=[pl.BlockSpec((tm, D_MODEL), lambda i: (i, 0)),
                  pl.BlockSpec((tm, o2d.shape[1]), lambda i: (i, 0))]
                 + [_layer_spec(a, layer, 1) for a in weights]
                 + [pl.BlockSpec(norm_f.shape, lambda i: (0, 0), pipeline_mode=pl.Buffered(1))],
        out_specs=pl.BlockSpec((tm, D_MODEL), lambda i: (i, 0)),
        out_shape=jax.ShapeDtypeStruct((n, D_MODEL), jnp.float32),
        compiler_params=pltpu.CompilerParams(
            dimension_semantics=("arbitrary",), vmem_limit_bytes=VMEM_LIMIT_BYTES,
            allow_input_fusion=[False] * 5 + [True, False, True, True, False]),
        name="channel",
    )(x2d, o2d, *weights, norm_f)


def _prep_weights(norm1, w_in, b_i, b_f, conv_w, conv_b, w_mq, w_mk, w_mv, m_skip, w_a2, b_a,
                  w_br_ret, w_br_mlstm, w_br_gla, w_out, norm2, w_ffn_in, w_ffn_out):
    f32, bf = jnp.float32, MXU_DTYPE
    depth = w_in.shape[0]
    offs = np.cumsum((0,) + IN_WIDTHS)
    win = w_in.astype(bf)
    rq, rk, rv, rg, mx, mz, mi, mf, gq, gk, gv, gr, ga, z_ret, z_ml, z_gla = (
        win[:, :, offs[i]:offs[i + 1]] for i in range(len(IN_WIDTHS)))

    def rotary_layout(wc):
        wc = wc.reshape(depth, D_MODEL, R_HEADS, 2, R_DK // 2)
        return jnp.swapaxes(wc, 2, 3).reshape(depth, D_MODEL, R_QK)

    def zcols(n):
        return jnp.zeros((depth, D_MODEL, n), bf)

    wsmall = jnp.concatenate([mi, zcols(SUBLANES - M_HEADS), mf, zcols(SUBLANES - M_HEADS), ga,
                              zcols(LANES - 2 * SUBLANES - G_RANK)], axis=2)
    wa2 = jnp.pad(w_a2, ((0, 0), (2 * SUBLANES, LANES - 2 * SUBLANES - G_RANK), (0, 0))).astype(bf)

    def gate_rows(b):
        return jnp.pad(jnp.broadcast_to(b.astype(f32)[:, :, None], (depth, M_HEADS, LANES)),
                       ((0, 0), (0, SUBLANES - M_HEADS), (0, 0)))

    return dict(
        g1=norm1[:, None, :], win=win, wrqk=jnp.concatenate([rotary_layout(rq), rotary_layout(rk)], axis=2),
        wsmall=wsmall, wgla=jnp.concatenate([gq, gk, gv, gr], axis=2), wa2=wa2, ba=b_a[:, None, :],
        gate_b=jnp.concatenate([gate_rows(b_i), gate_rows(b_f)], axis=1),
        conv_w=conv_w, conv_b=conv_b[:, None, :], wq=w_mq.astype(bf), wk=w_mk.astype(bf), wv=w_mv.astype(bf),
        skip=m_skip[:, None, :],
        wg=jnp.concatenate([z_ret, z_ml, z_gla], axis=2),
        wbr=jnp.concatenate([w_br_ret, w_br_mlstm, w_br_gla], axis=1).astype(bf),
        wout=w_out.astype(bf), g2=norm2[:, None, :], wfi=w_ffn_in.astype(bf), wfo=w_ffn_out.astype(bf))


def _rotary_tables(pos):
    half = R_DK // 2
    inv = 1.0 / (ROPE_BASE ** jnp.linspace(0.0, 1.0, half, dtype=jnp.float32))
    ang = pos[:, None] * inv[None, :]
    return jnp.tile(jnp.cos(ang), (1, R_HEADS)), jnp.tile(jnp.sin(ang), (1, R_HEADS))


def _m_rows(m):
    pad = [(0, 0)] * (m.ndim - 1) + [(0, SUBLANES - M_HEADS), (0, 0)]
    return jnp.pad(jnp.broadcast_to(m.astype(jnp.float32)[..., None], m.shape + (LANES,)), pad)


def _trunk(x, pos0, inits, w, norm_f, *, tb, tm):
    bsz, n, _ = x.shape
    t = -(-n // CHUNK) * CHUNK
    rows = bsz * n
    cos, sin = _rotary_tables(pos0 + jnp.arange(t, dtype=jnp.float32))
    states = []
    for l in range(DEPTH):
        xp = x if t == n else jnp.pad(x, ((0, 0), (0, t - n), (0, 0)))
        res = _mixer_call(xp, cos, sin, inits[l], w, l, nreal=n, tb=min(tb, t))
        states.append(tuple(res[1:]))
        o = res[0] if t == n else res[0][:, 0:n]
        x = _channel_call(x.reshape(rows, D_MODEL), o.reshape(rows, -1), w, l, norm_f,
                          final=(l == DEPTH - 1), tm=min(tm, rows)).reshape(bsz, n, D_MODEL)
    return x, states


def kernel(x_prompt, x_sample, state_ret, state_mlstm_c, state_mlstm_n, state_mlstm_m, state_mlstm_conv, state_gla, meta_tokens, norm1, w_in, b_mlstm_i, b_mlstm_f, conv_w, conv_b, w_mq, w_mk, w_mv, m_skip, w_gla_a2, b_gla_a, w_br_ret, w_br_mlstm, w_br_gla, w_out, norm2, w_ffn_in, w_ffn_out, norm_f):
    f32 = jnp.float32
    w = _prep_weights(norm1, w_in, b_mlstm_i, b_mlstm_f, conv_w, conv_b, w_mq, w_mk, w_mv, m_skip,
                      w_gla_a2, b_gla_a, w_br_ret, w_br_mlstm, w_br_gla, w_out, norm2, w_ffn_in, w_ffn_out)
    gf = norm_f[None]
    tiles = dict(tb=MIXER_BLOCK, tm=CHANNEL_TILE)

    _, meta_states = _trunk(meta_tokens.astype(f32)[None], 0.0, [None] * DEPTH, w, gf, **tiles)
    p_init = [(tuple(a[None] for a in st), 0) for st in meta_states]
    yp, p_states = _trunk(x_prompt, float(N_META), p_init, w, gf, **tiles)

    s_arrays = (state_ret, state_mlstm_c, state_mlstm_n, _m_rows(state_mlstm_m), state_mlstm_conv, state_gla)
    ys, s_states = _trunk(x_sample, float(N_META + PAST_LEN), [(s_arrays, l) for l in range(DEPTH)], w, gf,
                          **tiles)

    def stack(states):
        ret, c, n, m, conv, gla = (jnp.stack([st[i] for st in states]) for i in range(6))
        return ret, c, n, m[:, :, 0:M_HEADS, 0], conv, gla

    return (yp, ys) + stack(p_states) + stack(s_states)
```
